```python
import math
import jax, jax.numpy as jnp
from jax import lax
import numpy as np

D_MODEL = 4096
BATCH = 4
SEQ = 2048
DEPTH = 1
DEC_BATCH = 128
DEC_SEQ = 4
PAST_LEN = 16384
PAGE_SIZE = 128

S5_WIDTH = D_MODEL // 2
SSM_GROUP_CH = 16
SSM_GROUPS = S5_WIDTH // SSM_GROUP_CH
SSM_STATE = 64
HG_WIDTH = D_MODEL - S5_WIDTH
HG_HEAD_DIM = 128
HG_HEADS = HG_WIDTH // HG_HEAD_DIM
HG_CHUNK = 64
IN_COLS = S5_WIDTH + 4 * HG_WIDTH
N_EXPERTS = 32
TOP_K = 4
D_FF = D_MODEL
SWIGLU_LIMIT = 7.0
SWIGLU_ALPHA = 1.702
EXPERT_BLOCK = 128
NORM_EPS = 1e-5
F32 = jnp.float32

kernel_name = 'hymba_s5_hgrn2_moe_step'


def rmsnorm(x, g):
    xf = x.astype(F32)
    y = xf * lax.rsqrt(jnp.mean(xf * xf, axis=-1, keepdims=True) + NORM_EPS) * g.astype(F32)
    return y.astype(x.dtype)


def _ssm_combine(e1, e2):
    a1, b1 = e1
    a2, b2 = e2
    return a1 * a2, a2 * b1 + b2


def s5_mixer(u, h0_re, h0_im, a_re, a_im, log_dt, b_re, b_im, c_re, c_im, d_skip, w_glu, b_glu):
    Bsz, L, _ = u.shape
    uf = u.astype(F32).reshape(Bsz, L, SSM_GROUPS, SSM_GROUP_CH)
    lam = lax.complex(a_re.astype(F32), a_im.astype(F32))
    dt = jnp.exp(log_dt.astype(F32))[:, None]
    lam_bar = jnp.exp(lam * dt)
    b_mat = lax.complex(b_re.astype(F32), b_im.astype(F32))
    b_bar = ((lam_bar - 1.0) / lam)[..., None] * b_mat
    c_mat = lax.complex(c_re.astype(F32), c_im.astype(F32))
    bu = jnp.einsum('gpc,blgc->blgp', b_bar, uf.astype(jnp.complex64))
    h0 = lax.complex(h0_re.astype(F32), h0_im.astype(F32))
    bu = bu.at[:, 0].add(lam_bar * h0)
    a = jnp.broadcast_to(lam_bar, bu.shape)
    _, h = lax.associative_scan(_ssm_combine, (a, bu), axis=1)
    y = jnp.einsum('gcp,blgp->blgc', c_mat, h).real + d_skip.astype(F32) * uf
    y = y.reshape(Bsz, L, S5_WIDTH)
    z = jax.nn.gelu(y)
    out = z * jax.nn.sigmoid(z @ w_glu.astype(F32) + b_glu.astype(F32))
    h_last = h[:, -1]
    return out.astype(u.dtype), h_last.real, h_last.imag


def _hgrn2_chunk_step(S, inp):
    q, k, v, log_f = inp
    C = q.shape[2]
    b = jnp.cumsum(log_f, axis=2)
    causal = jnp.tril(jnp.ones((C, C), dtype=bool))
    diff = b[:, :, :, None, :] - b[:, :, None, :, :]
    decay = jnp.exp(jnp.where(causal[:, :, None], diff, -jnp.inf))
    scores = jnp.einsum('bhtk,bhsk,bhtsk->bhts', q, k, decay)
    o = jnp.einsum('bhts,bhsv->bhtv', scores, v) + jnp.einsum('bhtk,bhkv->bhtv', q * jnp.exp(b), S)
    b_last = b[:, :, -1:, :]
    S_new = jnp.exp(b_last[:, :, 0, :])[..., None] * S + jnp.einsum('bhck,bhcv->bhkv', k * jnp.exp(b_last - b), v)
    return S_new, o


def hgrn2_mixer(q_in, f_in, i_in, g_in, S0, lb, o_gain):
    Bsz, L, _ = q_in.shape

    def heads(t):
        return t.astype(F32).reshape(Bsz, L, HG_HEADS, HG_HEAD_DIM).transpose(0, 2, 1, 3)

    q = jax.nn.silu(heads(q_in))
    lb_ = lb.astype(F32)[None, :, None, :]
    f = lb_ + (1.0 - lb_) * jax.nn.sigmoid(heads(f_in))
    log_f = jnp.log(f)
    k = 1.0 - f
    v = heads(i_in)
    c = min(HG_CHUNK, L)
    n = -(-L // c)
    pad = n * c - L

    def chunks(t):
        t = jnp.pad(t, ((0, 0), (0, 0), (0, pad), (0, 0)))
        return t.reshape(Bsz, HG_HEADS, n, c, t.shape[-1]).transpose(2, 0, 1, 3, 4)

    S_last, o = lax.scan(_hgrn2_chunk_step, S0.astype(F32), (chunks(q), chunks(k), chunks(v), chunks(log_f)))
    o = o.transpose(1, 2, 0, 3, 4).reshape(Bsz, HG_HEADS, n * c, HG_HEAD_DIM)[:, :, :L]
    o = o * lax.rsqrt(jnp.mean(o * o, axis=-1, keepdims=True) + NORM_EPS) * o_gain.astype(F32)
    o = o.transpose(0, 2, 1, 3).reshape(Bsz, L, HG_WIDTH) * jax.nn.silu(g_in.astype(F32))
    return o.astype(q_in.dtype), S_last


def parallel_mixers(h, ssm_re0, ssm_im0, hg_S0, w_in, a_re, a_im, log_dt, b_re, b_im, c_re, c_im,
                    d_skip, w_glu, b_glu, lb, o_gain, w_out):
    proj = h @ w_in
    u, q, f, i, g = jnp.split(proj, [S5_WIDTH, S5_WIDTH + HG_WIDTH, S5_WIDTH + 2 * HG_WIDTH,
                                     S5_WIDTH + 3 * HG_WIDTH], axis=-1)
    s5_out, re, im = s5_mixer(u, ssm_re0, ssm_im0, a_re, a_im, log_dt, b_re, b_im, c_re, c_im,
                              d_skip, w_glu, b_glu)
    hg_out, S = hgrn2_mixer(q, f, i, g, hg_S0, lb, o_gain)
    y = jnp.concatenate([s5_out, hg_out], axis=-1) @ w_out
    return y, re, im, S


def routed_experts(xf, w_router, b_router, w_gate, b_gate, w_up, b_up, w_down, b_down):
    T, D = xf.shape
    logits = (xf @ w_router).astype(F32) + b_router.astype(F32)
    top_logit, top_idx = lax.top_k(logits, TOP_K)
    gates = jax.nn.softmax(top_logit, axis=-1)
    M = T * TOP_K
    flat_e = top_idx.reshape(M)
    order = jnp.argsort(flat_e)
    sorted_e = flat_e[order]
    counts = jnp.bincount(flat_e, length=N_EXPERTS)
    padded = (counts + EXPERT_BLOCK - 1) // EXPERT_BLOCK * EXPERT_BLOCK
    pad_end = jnp.cumsum(padded)
    pad_start = pad_end - padded
    start = jnp.cumsum(counts) - counts
    dest = pad_start[sorted_e] + jnp.arange(M) - start[sorted_e]
    n_blocks = -(-(M + N_EXPERTS * (EXPERT_BLOCK - 1)) // EXPERT_BLOCK)
    P = n_blocks * EXPERT_BLOCK
    row_tok = jnp.full((P,), T, jnp.int32).at[dest].set((order // TOP_K).astype(jnp.int32))
    row_gate = jnp.zeros((P,), F32).at[dest].set(gates.reshape(M)[order])
    x_rows = jnp.concatenate([xf, jnp.zeros((1, D), xf.dtype)], axis=0)[row_tok]
    x_rows = x_rows.reshape(n_blocks, EXPERT_BLOCK, D)
    block_e = jnp.minimum(jnp.searchsorted(pad_end, jnp.arange(n_blocks) * EXPERT_BLOCK, side='right'),
                          N_EXPERTS - 1)

    def expert_block(args):
        xb, e = args
        gate = (xb @ w_gate[e] + b_gate[e]).astype(F32)
        up = (xb @ w_up[e] + b_up[e]).astype(F32)
        gate = jnp.minimum(gate, SWIGLU_LIMIT)
        up = jnp.clip(up, -SWIGLU_LIMIT, SWIGLU_LIMIT)
        hid = (up + 1.0) * gate * jax.nn.sigmoid(SWIGLU_ALPHA * gate)
        return (hid.astype(xb.dtype) @ w_down[e] + b_down[e]).astype(F32)

    y_rows = lax.map(expert_block, (x_rows, block_e)).reshape(P, D)
    out = jnp.zeros((T + 1, D), F32).at[row_tok].add(y_rows * row_gate[:, None])[:T]
    return out.astype(xf.dtype)


def setup_inputs(seed: int = 0) -> dict:
    key = jax.random.key(seed)
    ks = iter(jax.random.split(key, 40))

    def nrm(shape, scale):
        return scale * jax.random.normal(next(ks), shape, F32)

    G, P, GC = SSM_GROUPS, SSM_STATE, SSM_GROUP_CH
    n_idx = jnp.arange(P, dtype=F32)
    return {
        'x_prompt': nrm((BATCH, SEQ, D_MODEL), 1.0),
        'x_sample': nrm((DEC_BATCH, DEC_SEQ, D_MODEL), 1.0),
        'state_ssm_re': nrm((DEPTH, DEC_BATCH, G, P), 0.5),
        'state_ssm_im': nrm((DEPTH, DEC_BATCH, G, P), 0.5),
        'state_hgrn': nrm((DEPTH, DEC_BATCH, HG_HEADS, HG_HEAD_DIM, HG_HEAD_DIM), 0.5),
        'g_mix': 1.0 + nrm((DEPTH, D_MODEL), 0.05),
        'w_in': nrm((DEPTH, D_MODEL, IN_COLS), D_MODEL ** -0.5),
        'ssm_a_re': -0.5 + nrm((DEPTH, G, P), 0.01),
        'ssm_a_im': math.pi * n_idx + nrm((DEPTH, G, P), 0.01),
        'ssm_log_dt': jax.random.uniform(next(ks), (DEPTH, G), F32, math.log(1e-3), math.log(1e-1)),
        'ssm_b_re': nrm((DEPTH, G, P, GC), (2 * GC) ** -0.5),
        'ssm_b_im': nrm((DEPTH, G, P, GC), (2 * GC) ** -0.5),
        'ssm_c_re': nrm((DEPTH, G, GC, P), (2 * P) ** -0.5),
        'ssm_c_im': nrm((DEPTH, G, GC, P), (2 * P) ** -0.5),
        'ssm_d': nrm((DEPTH, G, GC), 0.5),
        'w_glu': nrm((DEPTH, S5_WIDTH, S5_WIDTH), S5_WIDTH ** -0.5),
        'b_glu': nrm((DEPTH, S5_WIDTH), 0.02),
        'hg_lb_logits': nrm((DEPTH + 1, HG_HEADS, HG_HEAD_DIM), 0.5),
        'hg_o_gain': 1.0 + nrm((DEPTH, HG_HEAD_DIM), 0.05),
        'w_out': nrm((DEPTH, D_MODEL, D_MODEL), D_MODEL ** -0.5),
        'g_ffn': 1.0 + nrm((DEPTH, D_MODEL), 0.05),
        'w_router': nrm((DEPTH, D_MODEL, N_EXPERTS), D_MODEL ** -0.5),
        'b_router': nrm((DEPTH, N_EXPERTS), 0.01),
        'w_gate': nrm((DEPTH, N_EXPERTS, D_MODEL, D_FF), D_MODEL ** -0.5),
        'b_gate': nrm((DEPTH, N_EXPERTS, D_FF), 0.02),
        'w_up': nrm((DEPTH, N_EXPERTS, D_MODEL, D_FF), D_MODEL ** -0.5),
        'b_up': nrm((DEPTH, N_EXPERTS, D_FF), 0.02),
        'w_down': nrm((DEPTH, N_EXPERTS, D_FF, D_MODEL), D_FF ** -0.5),
        'b_down': nrm((DEPTH, N_EXPERTS, D_MODEL), 0.02),
        'g_final': 1.0 + nrm((D_MODEL,), 0.05),
    }


def reference(x_prompt, x_sample, state_ssm_re, state_ssm_im, state_hgrn, g_mix, w_in, ssm_a_re, ssm_a_im,
              ssm_log_dt, ssm_b_re, ssm_b_im, ssm_c_re, ssm_c_im, ssm_d, w_glu, b_glu, hg_lb_logits,
              hg_o_gain, w_out, g_ffn, w_router, b_router, w_gate, b_gate, w_up, b_up, w_down, b_down,
              g_final):
    Bp, Lp, _ = x_prompt.shape
    Bs, Ls, _ = x_sample.shape
    lower_bounds = jnp.cumsum(jax.nn.softmax(hg_lb_logits.astype(F32), axis=0), axis=0)
    zero_re = jnp.zeros((Bp, SSM_GROUPS, SSM_STATE), F32)
    zero_S = jnp.zeros((Bp, HG_HEADS, HG_HEAD_DIM, HG_HEAD_DIM), F32)
    xp, xs = x_prompt, x_sample
    re_p, im_p, S_p, re_s, im_s, S_s = [], [], [], [], [], []
    for l in range(DEPTH):
        def mix(h, re0, im0, S0):
            return parallel_mixers(h, re0, im0, S0, w_in[l], ssm_a_re[l], ssm_a_im[l], ssm_log_dt[l],
                                   ssm_b_re[l], ssm_b_im[l], ssm_c_re[l], ssm_c_im[l], ssm_d[l], w_glu[l],
                                   b_glu[l], lower_bounds[l], hg_o_gain[l], w_out[l])
        yp, rp, ip, Sp = mix(rmsnorm(xp, g_mix[l]), zero_re, zero_re, zero_S)
        ys, rs, is_, Ss = mix(rmsnorm(xs, g_mix[l]), state_ssm_re[l], state_ssm_im[l], state_hgrn[l])
        xp = xp + yp
        xs = xs + ys
        re_p.append(rp); im_p.append(ip); S_p.append(Sp)
        re_s.append(rs); im_s.append(is_); S_s.append(Ss)
        tokens = jnp.concatenate([rmsnorm(xp, g_ffn[l]).reshape(Bp * Lp, D_MODEL),
                                  rmsnorm(xs, g_ffn[l]).reshape(Bs * Ls, D_MODEL)], axis=0)
        ffn = routed_experts(tokens, w_router[l], b_router[l], w_gate[l], b_gate[l], w_up[l], b_up[l],
                             w_down[l], b_down[l])
        xp = xp + ffn[:Bp * Lp].reshape(xp.shape)
        xs = xs + ffn[Bp * Lp:].reshape(xs.shape)
    y_prompt = rmsnorm(xp, g_final)
    y_sample = rmsnorm(xs, g_final)
    sd = state_ssm_re.dtype
    new_ssm_re_prompt = jnp.stack(re_p).astype(sd)
    new_ssm_im_prompt = jnp.stack(im_p).astype(sd)
    new_hgrn_prompt = jnp.stack(S_p).astype(state_hgrn.dtype)
    new_ssm_re_sample = jnp.stack(re_s).astype(sd)
    new_ssm_im_sample = jnp.stack(im_s).astype(sd)
    new_hgrn_sample = jnp.stack(S_s).astype(state_hgrn.dtype)
    return (y_prompt, y_sample, new_ssm_re_prompt, new_ssm_im_prompt, new_hgrn_prompt,
            new_ssm_re_sample, new_ssm_im_sample, new_hgrn_sample)
```

```python
import functools
import math

import jax
import jax.numpy as jnp
from jax import lax
from jax.experimental import pallas as pl
from jax.experimental.pallas import tpu as pltpu

F32 = jnp.float32
BF16 = jnp.bfloat16
TOP_K = 4
NORM_EPS = 1e-5
SWIGLU_LIMIT = 7.0
SWIGLU_ALPHA = 1.702
LANES = 128
SUBLANES = 8
FOLD = SUBLANES
HG_CHUNK = 64
HG_SUB = 16
VMEM_LIMIT_BYTES = 56 * 1024 * 1024


def _cparams(n_axes):
    return pltpu.CompilerParams(dimension_semantics=("arbitrary",) * n_axes,
                                vmem_limit_bytes=VMEM_LIMIT_BYTES)


def _pick_tile(n, target, mult):
    best = None
    for t in range(mult, min(n, target) + 1, mult):
        if n % t == 0:
            best = t
    assert best is not None, (n, target, mult)
    return best


def _sigmoid(x):
    return 1.0 / (1.0 + jnp.exp(-x))


def _split3(x):
    h1 = x.astype(BF16)
    r1 = x - h1.astype(F32)
    h2 = r1.astype(BF16)
    r2 = r1 - h2.astype(F32)
    return h1, h2, r2.astype(BF16)


def _cast_rows(src_ref, dst_ref, chunk):
    rows = src_ref.shape[0]
    chunk = min(chunk, rows)
    assert rows % chunk == 0

    def body(r, c):
        sl = pl.ds(pl.multiple_of(r * chunk, chunk), chunk)
        dst_ref[sl, :] = src_ref[sl, :].astype(dst_ref.dtype)
        return c

    lax.fori_loop(0, rows // chunk, body, 0)


def _rmsnorm_kernel(x_ref, g_ref, o_ref):
    x = x_ref[...]
    ms = jnp.mean(x * x, axis=-1, keepdims=True)
    o_ref[...] = (x * lax.rsqrt(ms + NORM_EPS) * g_ref[...]).astype(o_ref.dtype)


def _rmsnorm(x, g, out_dtype):
    t, d = x.shape
    tm = _pick_tile(t, 256, 16)
    return pl.pallas_call(
        _rmsnorm_kernel,
        grid=(t // tm,),
        in_specs=[pl.BlockSpec((tm, d), lambda i: (i, 0)), pl.BlockSpec((1, d), lambda i: (0, 0))],
        out_specs=pl.BlockSpec((tm, d), lambda i: (i, 0)),
        out_shape=jax.ShapeDtypeStruct((t, d), out_dtype),
        compiler_params=_cparams(1),
        name="rmsnorm",
    )(x, g.reshape(1, d))


def _mm_kernel(*refs, n_a, has_res, fold_out):
    a_refs = refs[:n_a]
    w_ref = refs[n_a]
    pos = n_a + 1
    res_ref = None
    if has_res:
        res_ref = refs[pos]
        pos += 1
    o_ref = refs[pos]
    w_bf = refs[pos + 1]

    @pl.when(pl.program_id(1) == 0)
    def _():
        _cast_rows(w_ref, w_bf, 256)

    acc = None
    k0 = 0
    for a_ref in a_refs:
        ka = a_ref.shape[1]
        part = jnp.dot(a_ref[...], w_bf[k0:k0 + ka, :], preferred_element_type=F32)
        acc = part if acc is None else acc + part
        k0 += ka
    if has_res:
        acc = acc + res_ref[...]
    if not fold_out:
        o_ref[...] = acc.astype(o_ref.dtype)
    else:
        slab = refs[pos + 2]
        tm = acc.shape[0]
        for k in range(acc.shape[1] // LANES):
            slab[k] = acc[:, k * LANES:(k + 1) * LANES]
        for k in range(acc.shape[1] // LANES):
            for s in range(FOLD):
                o_ref[k, :, s * LANES:(s + 1) * LANES] = slab[k, pl.ds(s, tm // FOLD, stride=FOLD), :]


def _matmul(a_list, w, col0, ncols, *, res=None, fold_out=False, out_dtype=F32, tm_target=512, tn=512):
    t = a_list[0].shape[0]
    k_total = sum(a.shape[1] for a in a_list)
    assert w.shape[0] == k_total
    tn = math.gcd(math.gcd(ncols, col0), tn)
    assert tn % LANES == 0
    tm = _pick_tile(t, tm_target, SUBLANES * FOLD if fold_out else 16)
    nj, ni = ncols // tn, t // tm
    j0 = col0 // tn
    in_specs = [pl.BlockSpec((tm, a.shape[1]), lambda j, i: (i, 0)) for a in a_list]
    in_specs.append(pl.BlockSpec((k_total, tn), lambda j, i: (0, j + j0)))
    args = list(a_list) + [w]
    if res is not None:
        in_specs.append(pl.BlockSpec((tm, tn), lambda j, i: (i, j)))
        args.append(res)
    scratch = [pltpu.VMEM((k_total, tn), BF16)]
    if fold_out:
        nk = tn // LANES
        out_shape = jax.ShapeDtypeStruct((ncols // LANES, t // FOLD, FOLD * LANES), F32)
        out_spec = pl.BlockSpec((nk, tm // FOLD, FOLD * LANES), lambda j, i: (j, i, 0))
        scratch.append(pltpu.VMEM((nk, tm, LANES), F32))
    else:
        out_shape = jax.ShapeDtypeStruct((t, ncols), out_dtype)
        out_spec = pl.BlockSpec((tm, tn), lambda j, i: (i, j))
    return pl.pallas_call(
        functools.partial(_mm_kernel, n_a=len(a_list), has_res=res is not None, fold_out=fold_out),
        grid=(nj, ni),
        in_specs=in_specs,
        out_specs=out_spec,
        out_shape=out_shape,
        scratch_shapes=scratch,
        compiler_params=_cparams(2),
        name="dense_matmul",
    )(*args)


def _s5_tables(a_re, a_im, log_dt, b_re, b_im, c_re, c_im, d_skip, nseq):
    hi = lax.Precision.HIGHEST
    g, p, gc = b_re.shape
    gt = LANES // gc
    nt = g // gt
    cs = FOLD // nseq
    dt = jnp.exp(log_dt.astype(F32))[:, None]
    ar, ai = a_re.astype(F32) * dt, a_im.astype(F32) * dt
    mag = jnp.exp(ar)
    lr, li = mag * jnp.cos(ai), mag * jnp.sin(ai)
    den = a_re * a_re + a_im * a_im
    half = jnp.sin(0.5 * ai)
    nr, ni = jnp.expm1(ar) * jnp.cos(ai) - 2.0 * half * half, li
    fr, fi = (nr * a_re + ni * a_im) / den, (ni * a_re - nr * a_im) / den
    bbr = fr[..., None] * b_re - fi[..., None] * b_im
    bbi = fr[..., None] * b_im + fi[..., None] * b_re
    taus = jnp.arange(cs + 1, dtype=F32)[:, None, None]
    pmag = jnp.exp(taus * ar)
    pr, pi = pmag * jnp.cos(taus * ai), pmag * jnp.sin(taus * ai)
    wr = pr[..., None] * bbr - pi[..., None] * bbi
    wi = pr[..., None] * bbi + pi[..., None] * bbr
    kt = (jnp.einsum('gcp,tgpd->tgcd', c_re, wr[:cs], precision=hi)
          - jnp.einsum('gcp,tgpd->tgcd', c_im, wi[:cs], precision=hi))
    s_idx = jnp.arange(cs)[:, None]
    t_idx = jnp.arange(cs)[None, :]
    lag = jnp.clip(t_idx - s_idx, 0, cs - 1)
    kst = jnp.where((t_idx >= s_idx)[:, :, None, None, None], kt[lag], 0.0)
    kst = kst.reshape(cs, cs, nt, gt, gc, gc)
    eq = jnp.eye(nseq, dtype=F32)
    eg = jnp.eye(gt, dtype=F32)
    n_in = nseq * cs * gt * gc
    n_st = nseq * 2 * gt * p
    m_loc = jnp.einsum('qr,ab,stjacd->jqsadrtbc', eq, eg, kst).reshape(nt, n_in, n_in)
    er = wr[:cs][::-1].reshape(cs, nt, gt, p, gc)
    ei = wi[:cs][::-1].reshape(cs, nt, gt, p, gc)
    e2 = jnp.stack([er, ei], axis=-1)
    m_end = jnp.einsum('qr,ab,sjapdi->jqsadribp', eq, eg, e2).reshape(nt, n_in, n_st)
    cr = c_re[None] * pr[1:, :, None, :] - c_im[None] * pi[1:, :, None, :]
    ci = c_re[None] * pi[1:, :, None, :] + c_im[None] * pr[1:, :, None, :]
    c2 = jnp.stack([cr, -ci], axis=-1).reshape(cs, nt, gt, gc, p, 2)
    m_car = jnp.einsum('qr,ab,tjacpi->jqiaprtbc', eq, eg, c2).reshape(nt, n_st, n_in)
    dvec = jnp.broadcast_to(d_skip.astype(F32).reshape(nt, 1, 1, gt, gc), (nt, nseq, cs, gt, gc)).reshape(nt, 1, n_in)
    lam_r = jnp.broadcast_to(pr[cs].reshape(nt, 1, 1, gt * p), (nt, 1, nseq, gt * p)).reshape(nt, 1, nseq * gt * p)
    lam_i = jnp.broadcast_to(pi[cs].reshape(nt, 1, 1, gt * p), (nt, 1, nseq, gt * p)).reshape(nt, 1, nseq * gt * p)
    return m_loc.astype(BF16), m_end.astype(BF16), m_car.astype(BF16), dvec, lam_r, lam_i


def _gelu_tanh(x):
    return 0.5 * x * (1.0 + jnp.tanh(math.sqrt(2.0 / math.pi) * (x + 0.044715 * (x * x * x))))


def _s5_seq_kernel(u_ref, ml_ref, me_ref, mc_ref, dv_ref, lr_ref, li_ref, z_ref, st_ref, el_scr, hin_scr):
    u = u_ref[0]
    ub = u.astype(BF16)
    rows = u.shape[0]
    hw = lr_ref.shape[2]
    y = jnp.dot(ub, ml_ref[0], preferred_element_type=F32)
    el_scr[...] = jnp.dot(ub, me_ref[0], preferred_element_type=F32)
    lam_r, lam_i = lr_ref[0], li_ref[0]

    def body(n, carry):
        hr, hi_ = carry
        hin_scr[pl.ds(n, 1), 0:hw] = hr
        hin_scr[pl.ds(n, 1), hw:2 * hw] = hi_
        e = el_scr[pl.ds(n, 1), :]
        return (lam_r * hr - lam_i * hi_ + e[:, 0:hw], lam_r * hi_ + lam_i * hr + e[:, hw:2 * hw])

    zero = jnp.zeros((1, hw), F32)
    hr, hi_ = lax.fori_loop(0, rows, body, (zero, zero))
    st_ref[0, 0, :, 0:hw] = hr
    st_ref[0, 0, :, hw:2 * hw] = hi_
    y = y + jnp.dot(hin_scr[...].astype(BF16), mc_ref[0], preferred_element_type=F32) + dv_ref[0] * u
    z_ref[0] = _gelu_tanh(y)


def _s5_batch_kernel(zprev_ref, u_ref, h0_ref, ml_ref, me_ref, mc_ref, dv_ref, lr_ref, li_ref, z_ref, st_ref, *, nseq):
    del zprev_ref
    u = u_ref[0]
    ub = u.astype(BF16)
    h0 = h0_ref[0]
    hw = lr_ref.shape[2] // nseq
    y = jnp.dot(ub, ml_ref[0], preferred_element_type=F32)
    el = jnp.dot(ub, me_ref[0], preferred_element_type=F32)
    y = y + jnp.dot(h0.astype(BF16), mc_ref[0], preferred_element_type=F32) + dv_ref[0] * u
    z_ref[0] = _gelu_tanh(y)
    lam_r, lam_i = lr_ref[0], li_ref[0]
    for q in range(nseq):
        o = 2 * hw * q
        hr, hi_ = h0[:, o:o + hw], h0[:, o + hw:o + 2 * hw]
        lrq, liq = lam_r[:, q * hw:(q + 1) * hw], lam_i[:, q * hw:(q + 1) * hw]
        st_ref[0, :, o:o + hw] = lrq * hr - liq * hi_ + el[:, o:o + hw]
        st_ref[0, :, o + hw:o + 2 * hw] = lrq * hi_ + liq * hr + el[:, o + hw:o + 2 * hw]


def _s5_mixer(u_fold, ssm, h0_re, h0_im, bp, lp, bs, ls):
    a_re, a_im, log_dt, b_re, b_im, c_re, c_im, d_skip = ssm
    g, p, gc = b_re.shape
    gt = LANES // gc
    nt, tf, wf = u_fold.shape
    assert g % gt == 0 and nt == g // gt and wf == FOLD * LANES
    assert lp % FOLD == 0 and FOLD % ls == 0
    nseq = FOLD // ls
    assert bs % nseq == 0
    rp = lp // FOLD
    rs = bs // nseq
    tp_f = bp * rp
    assert tp_f % rs == 0 and tf == tp_f + rs
    hw = gt * p

    tabs = _s5_tables(a_re, a_im, log_dt, b_re, b_im, c_re, c_im, d_skip, 1)
    tile = lambda shape: pl.BlockSpec((1,) + shape, lambda j, b: (j, 0, 0))
    z_p, st_p = pl.pallas_call(
        _s5_seq_kernel,
        grid=(nt, bp),
        in_specs=[pl.BlockSpec((1, rp, wf), lambda j, b: (j, b, 0)),
                  tile((wf, wf)), tile((wf, 2 * hw)), tile((2 * hw, wf)),
                  tile((1, wf)), tile((1, hw)), tile((1, hw))],
        out_specs=[pl.BlockSpec((1, rp, wf), lambda j, b: (j, b, 0)),
                   pl.BlockSpec((1, 1, 1, 2 * hw), lambda j, b: (j, b, 0, 0))],
        out_shape=[jax.ShapeDtypeStruct((nt, tf, wf), F32),
                   jax.ShapeDtypeStruct((nt, bp, 1, 2 * hw), F32)],
        scratch_shapes=[pltpu.VMEM((rp, 2 * hw), F32), pltpu.VMEM((rp, 2 * hw), F32)],
        compiler_params=_cparams(2),
        name="s5_prompt",
    )(u_fold, *tabs)

    tabs_s = _s5_tables(a_re, a_im, log_dt, b_re, b_im, c_re, c_im, d_skip, nseq)
    h0 = jnp.stack([h0_re.astype(F32), h0_im.astype(F32)], axis=1)
    h0 = h0.reshape(rs, nseq, 2, nt, gt * p).transpose(3, 0, 1, 2, 4).reshape(nt, rs, nseq * 2 * hw)
    sblk = tp_f // rs
    tile1 = lambda shape: pl.BlockSpec((1,) + shape, lambda j: (j, 0, 0))
    z_all, st_s = pl.pallas_call(
        functools.partial(_s5_batch_kernel, nseq=nseq),
        grid=(nt,),
        in_specs=[pl.BlockSpec(memory_space=pl.ANY),
                  pl.BlockSpec((1, rs, wf), lambda j: (j, sblk, 0)),
                  tile1((rs, nseq * 2 * hw)),
                  tile1((wf, wf)), tile1((wf, nseq * 2 * hw)), tile1((nseq * 2 * hw, wf)),
                  tile1((1, wf)), tile1((1, nseq * hw)), tile1((1, nseq * hw))],
        out_specs=[pl.BlockSpec((1, rs, wf), lambda j: (j, sblk, 0)),
                   tile1((rs, nseq * 2 * hw))],
        out_shape=[jax.ShapeDtypeStruct((nt, tf, wf), F32),
                   jax.ShapeDtypeStruct((nt, rs, nseq * 2 * hw), F32)],
        input_output_aliases={0: 0},
        compiler_params=_cparams(1),
        name="s5_sample",
    )(z_p, u_fold, h0, *tabs_s)

    st_p = st_p.reshape(nt, bp, 2, gt, p).transpose(2, 1, 0, 3, 4).reshape(2, bp, g, p)
    st_s = st_s.reshape(nt, rs, nseq, 2, gt, p).transpose(3, 1, 2, 0, 4, 5).reshape(2, bs, g, p)
    return z_all, st_p[0], st_p[1], st_s[0], st_s[1]


def _glu_kernel(zf_ref, w_ref, b_ref, o_ref, w_bf, z_nat):
    j = pl.program_id(0)

    @pl.when(pl.program_id(1) == 0)
    def _():
        _cast_rows(w_ref, w_bf, 256)

    nt, rf, _ = zf_ref.shape
    tm = rf * FOLD
    for k in range(nt):
        for t in range(FOLD):
            z_nat[k, pl.ds(t, rf, stride=FOLD), :] = zf_ref[k, :, t * LANES:(t + 1) * LANES]
    z = jnp.concatenate([z_nat[k] for k in range(nt)], axis=-1)
    tn = o_ref.shape[1]
    acc = jnp.dot(z.astype(BF16), w_bf[...], preferred_element_type=F32) + b_ref[...]
    nk = tn // LANES
    for kk in range(nk):
        zt = z_nat[j * nk + kk]
        o_ref[:, kk * LANES:(kk + 1) * LANES] = (zt * _sigmoid(acc[:, kk * LANES:(kk + 1) * LANES])).astype(o_ref.dtype)
    del tm


def _glu(z_fold, w_glu, b_glu):
    nt, tf, wf = z_fold.shape
    t = tf * FOLD
    n = nt * LANES
    tn = min(512, n)
    tm = _pick_tile(t, 512, SUBLANES * FOLD)
    return pl.pallas_call(
        _glu_kernel,
        grid=(n // tn, t // tm),
        in_specs=[pl.BlockSpec((nt, tm // FOLD, wf), lambda j, i: (0, i, 0)),
                  pl.BlockSpec((n, tn), lambda j, i: (0, j)),
                  pl.BlockSpec((1, tn), lambda j, i: (0, j))],
        out_specs=pl.BlockSpec((tm, tn), lambda j, i: (i, j)),
        out_shape=jax.ShapeDtypeStruct((t, n), BF16),
        scratch_shapes=[pltpu.VMEM((n, tn), BF16), pltpu.VMEM((nt, tm, LANES), F32)],
        compiler_params=_cparams(2),
        name="s5_glu",
    )(z_fold, w_glu, b_glu.reshape(1, n))


def _hg_gates(qin, fin, lb):
    q = qin * _sigmoid(qin)
    fg = lb + (1.0 - lb) * _sigmoid(fin)
    return q, fg, jnp.log(fg), 1.0 - fg


def _cumsum_rows(logf, tri_bf):
    h1, h2, h3 = _split3(logf)
    return (jnp.dot(tri_bf, h1, preferred_element_type=F32) + jnp.dot(tri_bf, h2, preferred_element_type=F32)
            + jnp.dot(tri_bf, h3, preferred_element_type=F32))


def _hg_out(o, gin, gain):
    o = o * lax.rsqrt(jnp.mean(o * o, axis=-1, keepdims=True) + NORM_EPS) * gain
    return o * (gin * _sigmoid(gin))


def _hg_prompt_kernel(q_ref, f_ref, i_ref, g_ref, lb_ref, gain_ref, o_ref, s_ref, st):
    n = pl.program_id(2)
    c, sub = HG_CHUNK, HG_SUB
    nsub = c // sub

    @pl.when(n == 0)
    def _():
        st[...] = jnp.zeros_like(st)

    lb = lb_ref[0]
    gain = gain_ref[...]
    ri = lax.broadcasted_iota(jnp.int32, (c, c), 0)
    ci = lax.broadcasted_iota(jnp.int32, (c, c), 1)
    tri = (ri >= ci).astype(BF16)
    t_io = lax.broadcasted_iota(jnp.int32, (sub, LANES), 0)
    nt_dims = (((1,), (1,)), ((), ()))
    tn_dims = (((0,), (0,)), ((), ()))

    for cc in range(q_ref.shape[0] // c):
        rows = slice(cc * c, (cc + 1) * c)
        q, fg, logf, k = _hg_gates(q_ref[rows, :], f_ref[rows, :], lb)
        v = i_ref[rows, :]
        b = _cumsum_rows(logf, tri)
        vb = v.astype(BF16)
        s_t = st[...]
        o_parts = []
        for i in range(nsub):
            sl = slice(i * sub, (i + 1) * sub)
            b_i, q_i, k_i, v_i = b[sl], q[sl], k[sl], v[sl]
            r_i = b[i * sub - 1:i * sub] if i > 0 else jnp.zeros((1, LANES), F32)
            o_i = jnp.zeros((sub, LANES), F32)
            for s in range(sub):
                arg = jnp.where(t_io >= s, b_i - b_i[s:s + 1], -jnp.inf)
                w = q_i * k_i[s:s + 1] * jnp.exp(arg)
                col = jnp.sum(w, axis=-1, keepdims=True)
                o_i = o_i + col * v_i[s:s + 1]
            if i > 0:
                qt = (q_i * jnp.exp(b_i - r_i)).astype(BF16)
                kfull = (k[0:i * sub] * jnp.exp(r_i - b[0:i * sub])).astype(BF16)
                a = lax.dot_general(qt, kfull, nt_dims, preferred_element_type=F32)
                o_i = o_i + jnp.dot(a.astype(BF16), vb[0:i * sub], preferred_element_type=F32)
            o_parts.append(o_i)
        o = jnp.concatenate(o_parts, axis=0)
        qe = (q * jnp.exp(b)).astype(BF16)
        o = o + lax.dot_general(qe, s_t.astype(BF16), nt_dims, preferred_element_type=F32)
        b_last = b[c - 1:c]
        khat = (k * jnp.exp(b_last - b)).astype(BF16)
        st[...] = s_t * jnp.exp(b_last) + lax.dot_general(vb, khat, tn_dims, preferred_element_type=F32)
        o_ref[rows, :] = _hg_out(o, g_ref[rows, :], gain).astype(o_ref.dtype)

    @pl.when(n == pl.num_programs(2) - 1)
    def _():
        s_ref[0, 0] = st[...].T


def _hg_sample_kernel(hprev_ref, q_ref, f_ref, i_ref, g_ref, lb_ref, gain_ref, s0_ref, o_ref, s_ref, *, ls):
    del hprev_ref
    nseq = s0_ref.shape[0]
    rows = nseq * ls
    lb = lb_ref[0]
    q, fg, logf, k = _hg_gates(q_ref[...], f_ref[...], lb)
    v = i_ref[...]
    ri = lax.broadcasted_iota(jnp.int32, (rows, rows), 0)
    ci = lax.broadcasted_iota(jnp.int32, (rows, rows), 1)
    tri = ((ri >= ci) & (ri // ls == ci // ls)).astype(BF16)
    b = _cumsum_rows(logf, tri)
    step = lax.broadcasted_iota(jnp.int32, (rows, LANES), 0) % ls
    o = jnp.zeros((rows, LANES), F32)
    for d in range(ls):
        if d == 0:
            ks, bs_, vs = k, b, v
        else:
            ks, bs_, vs = pltpu.roll(k, d, 0), pltpu.roll(b, d, 0), pltpu.roll(v, d, 0)
        arg = jnp.where(step >= d, b - bs_, -jnp.inf)
        col = jnp.sum(q * ks * jnp.exp(arg), axis=-1, keepdims=True)
        o = o + col * vs
    qe = (q * jnp.exp(b)).astype(BF16)
    tn_dims = (((0,), (0,)), ((), ()))
    last = ((ci == (ri // ls) * ls + ls - 1)).astype(BF16)
    b_end = _cumsum_rows(b, last)
    khat = k * jnp.exp(b_end - b)
    e_end = jnp.exp(b_end)
    seq_of_row = lax.broadcasted_iota(jnp.int32, (rows, LANES), 0) // ls
    erow = lax.broadcasted_iota(jnp.int32, (2 * SUBLANES, LANES), 0)
    ones = jnp.ones((2 * SUBLANES, LANES), BF16)
    vb = v.astype(BF16)
    for r in range(nseq):
        mine = seq_of_row == r
        s0 = s0_ref[r, 0]
        o = o + jnp.where(mine, jnp.dot(qe, s0.astype(BF16), preferred_element_type=F32), 0.0)
        kv = lax.dot_general(jnp.where(mine, khat, 0.0).astype(BF16), vb, tn_dims, preferred_element_type=F32)
        e_r = e_end[r * ls:r * ls + 1]
        e_hi = e_r.astype(BF16).astype(F32)
        emat = jnp.where(erow == 0, e_hi, jnp.where(erow == 1, e_r - e_hi, 0.0)).astype(BF16)
        decay = lax.dot_general(emat, ones, tn_dims, preferred_element_type=F32)
        s_ref[r, 0] = s0 * decay + kv
    o_ref[...] = _hg_out(o, g_ref[...], gain_ref[...]).astype(o_ref.dtype)


def _hgrn_mixer(qfig, lb, gain, s0, bp, lp, bs, ls):
    t = qfig.shape[0]
    h, dk = lb.shape
    assert dk == LANES and qfig.shape[1] == 4 * h * dk
    rb = _pick_tile(lp, 256, HG_CHUNK)
    nb = lp // rb
    lb3 = lb.reshape(h, 1, dk)
    gain2 = gain.reshape(1, dk)
    col = lambda off: pl.BlockSpec((rb, dk), lambda b, hh, n: (b * nb + n, off * h + hh))
    out_p, s_p = pl.pallas_call(
        _hg_prompt_kernel,
        grid=(bp, h, nb),
        in_specs=[col(0), col(1), col(2), col(3),
                  pl.BlockSpec((1, 1, dk), lambda b, hh, n: (hh, 0, 0)),
                  pl.BlockSpec((1, dk), lambda b, hh, n: (0, 0))],
        out_specs=[pl.BlockSpec((rb, dk), lambda b, hh, n: (b * nb + n, hh)),
                   pl.BlockSpec((1, 1, dk, dk), lambda b, hh, n: (b, hh, 0, 0))],
        out_shape=[jax.ShapeDtypeStruct((t, h * dk), BF16),
                   jax.ShapeDtypeStruct((bp, h, dk, dk), F32)],
        scratch_shapes=[pltpu.VMEM((dk, dk), F32)],
        compiler_params=_cparams(3),
        name="hgrn_prompt",
    )(qfig, qfig, qfig, qfig, lb3, gain2)

    sb = _pick_tile(bs, 8, 1)
    rows = sb * ls
    assert rows % 16 == 0 and (bp * lp) % rows == 0
    r0 = (bp * lp) // rows
    scol = lambda off: pl.BlockSpec((rows, dk), lambda i, hh: (r0 + i, off * h + hh))
    out_all, s_s = pl.pallas_call(
        functools.partial(_hg_sample_kernel, ls=ls),
        grid=(bs // sb, h),
        in_specs=[pl.BlockSpec(memory_space=pl.ANY),
                  scol(0), scol(1), scol(2), scol(3),
                  pl.BlockSpec((1, 1, dk), lambda i, hh: (hh, 0, 0)),
                  pl.BlockSpec((1, dk), lambda i, hh: (0, 0)),
                  pl.BlockSpec((sb, 1, dk, dk), lambda i, hh: (i, hh, 0, 0))],
        out_specs=[pl.BlockSpec((rows, dk), lambda i, hh: (r0 + i, hh)),
                   pl.BlockSpec((sb, 1, dk, dk), lambda i, hh: (i, hh, 0, 0))],
        out_shape=[jax.ShapeDtypeStruct((t, h * dk), BF16),
                   jax.ShapeDtypeStruct((bs, h, dk, dk), F32)],
        input_output_aliases={0: 0},
        compiler_params=_cparams(2),
        name="hgrn_sample",
    )(out_p, qfig, qfig, qfig, qfig, lb3, gain2, s0)
    return out_all, s_p, s_s


def _router_kernel(x_ref, g_ref, w_ref, b_ref, t_ref, idx_ref, gate_ref):
    x = x_ref[...]
    ms = jnp.mean(x * x, axis=-1, keepdims=True)
    tok = x * lax.rsqrt(ms + NORM_EPS) * g_ref[...]
    t_ref[...] = tok
    t_hi = tok.astype(BF16)
    t_lo = (tok - t_hi.astype(F32)).astype(BF16)
    w = w_ref[...]
    w_hi = w.astype(BF16)
    w_lo = (w - w_hi.astype(F32)).astype(BF16)
    logits = (jnp.dot(t_hi, w_hi, preferred_element_type=F32) + jnp.dot(t_hi, w_lo, preferred_element_type=F32)
              + jnp.dot(t_lo, w_hi, preferred_element_type=F32)) + b_ref[...]
    ne = logits.shape[1]
    lane = lax.broadcasted_iota(jnp.int32, logits.shape, 1).astype(F32)
    out_lane = lax.broadcasted_iota(jnp.int32, idx_ref.shape, 1)
    idx_out = jnp.zeros(idx_ref.shape, jnp.int32)
    val_out = jnp.zeros(idx_ref.shape, F32)
    vals = []
    cur = logits
    for r in range(TOP_K):
        m = jnp.max(cur, axis=-1, keepdims=True)
        sel = jnp.min(jnp.where(cur == m, lane, float(ne)), axis=-1, keepdims=True)
        cur = jnp.where(lane == sel, -jnp.inf, cur)
        vals.append(m)
        idx_out = jnp.where(out_lane == r, sel.astype(jnp.int32), idx_out)
    denom = sum(jnp.exp(v - vals[0]) for v in vals)
    for r in range(TOP_K):
        val_out = jnp.where(out_lane == r, jnp.exp(vals[r] - vals[0]) / denom, val_out)
    idx_ref[...] = idx_out
    gate_ref[...] = val_out


def _router(x1, g_ffn, w_router, b_router):
    t, d = x1.shape
    ne = w_router.shape[1]
    tm = _pick_tile(t, 256, 16)
    return pl.pallas_call(
        _router_kernel,
        grid=(t // tm,),
        in_specs=[pl.BlockSpec((tm, d), lambda i: (i, 0)), pl.BlockSpec((1, d), lambda i: (0, 0)),
                  pl.BlockSpec((d, ne), lambda i: (0, 0)), pl.BlockSpec((1, ne), lambda i: (0, 0))],
        out_specs=[pl.BlockSpec((tm, d), lambda i: (i, 0)), pl.BlockSpec((tm, LANES), lambda i: (i, 0)),
                   pl.BlockSpec((tm, LANES), lambda i: (i, 0))],
        out_shape=[jax.ShapeDtypeStruct((t, d), F32), jax.ShapeDtypeStruct((t, LANES), jnp.int32),
                   jax.ShapeDtypeStruct((t, LANES), F32)],
        compiler_params=_cparams(1),
        name="router",
    )(x1, g_ffn.reshape(1, d), w_router, b_router.reshape(1, ne))


def _row_copy(src_hbm, dst_vmem, src_row, dst_row, sem):
    return pltpu.make_async_copy(src_hbm.at[pl.ds(src_row, 1), :], dst_vmem.at[pl.ds(dst_row, 1), :], sem)


def _gather_kernel(tok_tab, src_hbm, o_ref, buf, sem):
    bm = buf.shape[0]
    base = pl.program_id(0) * bm

    def start(r, c):
        _row_copy(src_hbm, buf, tok_tab[base + r], r, sem).start()
        return c

    def wait(r, c):
        _row_copy(src_hbm, buf, 0, r, sem).wait()
        return c

    lax.fori_loop(0, bm, start, 0)
    lax.fori_loop(0, bm, wait, 0)
    o_ref[...] = buf[...].astype(o_ref.dtype)


def _gather_rows(src, row_tok, bm):
    p = row_tok.shape[0]
    d = src.shape[1]
    return pl.pallas_call(
        _gather_kernel,
        grid_spec=pltpu.PrefetchScalarGridSpec(
            num_scalar_prefetch=1,
            grid=(p // bm,),
            in_specs=[pl.BlockSpec(memory_space=pl.ANY)],
            out_specs=pl.BlockSpec((bm, d), lambda i, tab: (i, 0)),
            scratch_shapes=[pltpu.VMEM((bm, d), src.dtype), pltpu.SemaphoreType.DMA(())],
        ),
        out_shape=jax.ShapeDtypeStruct((p, d), BF16),
        compiler_params=_cparams(1),
        name="moe_gather",
    )(row_tok, src)


def _moe_up_kernel(e_tab, j_tab, rb_tab, first_tab, valid_tab, x_ref, wg_ref, wu_ref, bg_ref, bu_ref, o_ref, wg_bf, wu_bf):
    s = pl.program_id(0)

    @pl.when(first_tab[s] == 1)
    def _():
        _cast_rows(wg_ref, wg_bf, 256)
        _cast_rows(wu_ref, wu_bf, 256)

    @pl.when(valid_tab[s] == 1)
    def _():
        x = x_ref[...]
        gate = jnp.dot(x, wg_bf[...], preferred_element_type=F32) + bg_ref[...]
        up = jnp.dot(x, wu_bf[...], preferred_element_type=F32) + bu_ref[...]
        gate = jnp.minimum(gate, SWIGLU_LIMIT)
        up = jnp.clip(up, -SWIGLU_LIMIT, SWIGLU_LIMIT)
        hid = (up + 1.0) * gate * _sigmoid(SWIGLU_ALPHA * gate)
        o_ref[...] = hid.astype(o_ref.dtype)


def _moe_down_kernel(e_tab, j_tab, rb_tab, first_tab, valid_tab, h_ref, wd_ref, bd_ref, o_ref, wd_bf):
    s = pl.program_id(0)

    @pl.when(first_tab[s] == 1)
    def _():
        _cast_rows(wd_ref, wd_bf, 256)

    @pl.when(valid_tab[s] == 1)
    def _():
        o_ref[...] = jnp.dot(h_ref[...], wd_bf[...], preferred_element_type=F32) + bd_ref[...]


def _moe_tables(nb_e, nj, n_steps):
    ne = nb_e.shape[0]
    blk_end = jnp.cumsum(nb_e)
    blk_start = blk_end - nb_e
    total = blk_end[-1]
    s = jnp.minimum(jnp.arange(n_steps, dtype=jnp.int32), nj * total - 1)
    e_s = jnp.minimum(jnp.searchsorted(nj * blk_end, s, side='right'), ne - 1).astype(jnp.int32)
    local = s - nj * blk_start[e_s]
    nbe = jnp.maximum(nb_e[e_s], 1)
    j_s = (local // nbe).astype(jnp.int32)
    rb_s = (blk_start[e_s] + local % nbe).astype(jnp.int32)
    valid = (jnp.arange(n_steps) < nj * total).astype(jnp.int32)
    prev_e = jnp.concatenate([jnp.full((1,), -1, jnp.int32), e_s[:-1]])
    prev_j = jnp.concatenate([jnp.full((1,), -1, jnp.int32), j_s[:-1]])
    first = (((e_s != prev_e) | (j_s != prev_j)) & (valid == 1)).astype(jnp.int32)
    return e_s, j_s, rb_s, first, valid


def _moe_experts(xs, nb_e, w_gate, b_gate, w_up, b_up, w_down, b_down, bm):
    p, d = xs.shape
    ne, _, f = w_gate.shape
    tn = min(512, f)
    nb = p // bm
    nj = f // tn
    tabs = _moe_tables(nb_e, nj, nj * nb)
    wspec = lambda kdim: pl.BlockSpec((None, kdim, tn), lambda s, e, j, rb, fi, va: (e[s], 0, j[s]))
    bspec = pl.BlockSpec((None, 1, tn), lambda s, e, j, rb, fi, va: (e[s], 0, j[s]))
    hid = pl.pallas_call(
        _moe_up_kernel,
        grid_spec=pltpu.PrefetchScalarGridSpec(
            num_scalar_prefetch=5,
            grid=(nj * nb,),
            in_specs=[pl.BlockSpec((bm, d), lambda s, e, j, rb, fi, va: (rb[s], 0)),
                      wspec(d), wspec(d), bspec, bspec],
            out_specs=pl.BlockSpec((bm, tn), lambda s, e, j, rb, fi, va: (rb[s], j[s])),
            scratch_shapes=[pltpu.VMEM((d, tn), BF16), pltpu.VMEM((d, tn), BF16)],
        ),
        out_shape=jax.ShapeDtypeStruct((p, f), BF16),
        compiler_params=_cparams(1),
        name="moe_up",
    )(*tabs, xs, w_gate, w_up, b_gate.reshape(ne, 1, f), b_up.reshape(ne, 1, f))
    tn2 = min(512, d)
    nj2 = d // tn2
    tabs2 = _moe_tables(nb_e, nj2, nj2 * nb)
    y = pl.pallas_call(
        _moe_down_kernel,
        grid_spec=pltpu.PrefetchScalarGridSpec(
            num_scalar_prefetch=5,
            grid=(nj2 * nb,),
            in_specs=[pl.BlockSpec((bm, f), lambda s, e, j, rb, fi, va: (rb[s], 0)),
                      pl.BlockSpec((None, f, tn2), lambda s, e, j, rb, fi, va: (e[s], 0, j[s])),
                      pl.BlockSpec((None, 1, tn2), lambda s, e, j, rb, fi, va: (e[s], 0, j[s]))],
            out_specs=pl.BlockSpec((bm, tn2), lambda s, e, j, rb, fi, va: (rb[s], j[s])),
            scratch_shapes=[pltpu.VMEM((f, tn2), BF16)],
        ),
        out_shape=jax.ShapeDtypeStruct((p, d), F32),
        compiler_params=_cparams(1),
        name="moe_down",
    )(*tabs2, hid, w_down, b_down.reshape(ne, 1, d))
    return y


def _combine_kernel(pos_tab, y_hbm, x_ref, gate_ref, g_ref, o_ref, buf, sem):
    tc = x_ref.shape[0]
    base = pl.program_id(0) * tc * TOP_K

    def start(r, c):
        for k in range(TOP_K):
            _row_copy(y_hbm, buf.at[k], pos_tab[base + r * TOP_K + k], r, sem).start()
        return c

    def wait(r, c):
        for k in range(TOP_K):
            _row_copy(y_hbm, buf.at[k], 0, r, sem).wait()
        return c

    lax.fori_loop(0, tc, start, 0)
    lax.fori_loop(0, tc, wait, 0)
    gates = gate_ref[...]
    acc = x_ref[...]
    for k in range(TOP_K):
        acc = acc + gates[:, k:k + 1] * buf[k]
    ms = jnp.mean(acc * acc, axis=-1, keepdims=True)
    o_ref[...] = acc * lax.rsqrt(ms + NORM_EPS) * g_ref[...]


def _combine(y, pos, x1, gates, g_final):
    t, d = x1.shape
    tc = _pick_tile(t, 64, 8)
    return pl.pallas_call(
        _combine_kernel,
        grid_spec=pltpu.PrefetchScalarGridSpec(
            num_scalar_prefetch=1,
            grid=(t // tc,),
            in_specs=[pl.BlockSpec(memory_space=pl.ANY),
                      pl.BlockSpec((tc, d), lambda i, tab: (i, 0)),
                      pl.BlockSpec((tc, LANES), lambda i, tab: (i, 0)),
                      pl.BlockSpec((1, d), lambda i, tab: (0, 0))],
            out_specs=pl.BlockSpec((tc, d), lambda i, tab: (i, 0)),
            scratch_shapes=[pltpu.VMEM((TOP_K, tc, d), F32), pltpu.SemaphoreType.DMA(())],
        ),
        out_shape=jax.ShapeDtypeStruct((t, d), F32),
        compiler_params=_cparams(1),
        name="moe_combine",
    )(pos, y, x1, gates, g_final.reshape(1, d))


def kernel(x_prompt, x_sample, state_ssm_re, state_ssm_im, state_hgrn, g_mix, w_in, ssm_a_re, ssm_a_im, ssm_log_dt, ssm_b_re, ssm_b_im, ssm_c_re, ssm_c_im, ssm_d, w_glu, b_glu, hg_lb_logits, hg_o_gain, w_out, g_ffn, w_router, b_router, w_gate, b_gate, w_up, b_up, w_down, b_down, g_final):
    bp, lp, d = x_prompt.shape
    bs, ls, _ = x_sample.shape
    depth = w_in.shape[0]
    assert depth == 1
    tp, ts = bp * lp, bs * ls
    t = tp + ts
    s5w = ssm_b_re.shape[1] * ssm_b_re.shape[3]
    h, dk = hg_lb_logits.shape[1:]
    ne = w_router.shape[2]

    x0 = jnp.concatenate([x_prompt.reshape(tp, d), x_sample.reshape(ts, d)], axis=0)
    lower_bounds = jnp.cumsum(jax.nn.softmax(hg_lb_logits.astype(F32), axis=0), axis=0)

    hn = _rmsnorm(x0, g_mix[0], BF16)
    u_fold = _matmul([hn], w_in[0], 0, s5w, fold_out=True)
    qfig = _matmul([hn], w_in[0], s5w, w_in.shape[2] - s5w)
    ssm = (ssm_a_re[0], ssm_a_im[0], ssm_log_dt[0], ssm_b_re[0], ssm_b_im[0], ssm_c_re[0], ssm_c_im[0], ssm_d[0])
    z_fold, re_p, im_p, re_s, im_s = _s5_mixer(u_fold, ssm, state_ssm_re[0], state_ssm_im[0], bp, lp, bs, ls)
    s5_out = _glu(z_fold, w_glu[0], b_glu[0])
    hg_out, s_p, s_s = _hgrn_mixer(qfig, lower_bounds[0], hg_o_gain[0], state_hgrn[0], bp, lp, bs, ls)
    x1 = _matmul([s5_out, hg_out], w_out[0], 0, d, res=x0)

    tok, idx, gates = _router(x1, g_ffn[0], w_router[0], b_router[0])
    m = t * TOP_K
    bm = 256
    flat_e = idx[:, :TOP_K].reshape(m)
    order = jnp.argsort(flat_e).astype(jnp.int32)
    sorted_e = flat_e[order]
    counts = jnp.bincount(flat_e, length=ne).astype(jnp.int32)
    nb_e = (counts + bm - 1) // bm
    pad_end = jnp.cumsum(nb_e * bm)
    pad_start = pad_end - nb_e * bm
    start = jnp.cumsum(counts) - counts
    dest = (pad_start[sorted_e] + jnp.arange(m, dtype=jnp.int32) - start[sorted_e]).astype(jnp.int32)
    n_blocks = -(-(m + ne * (bm - 1)) // bm)
    p_rows = n_blocks * bm
    row_tok = jnp.zeros((p_rows,), jnp.int32).at[dest].set(order // TOP_K)
    pos = jnp.zeros((m,), jnp.int32).at[order].set(dest)
    xs = _gather_rows(tok, row_tok, bm)
    y_rows = _moe_experts(xs, nb_e, w_gate[0], b_gate[0], w_up[0], b_up[0], w_down[0], b_down[0], bm)
    y_all = _combine(y_rows, pos, x1, gates, g_final)

    y_prompt = y_all[:tp].reshape(bp, lp, d)
    y_sample = y_all[tp:].reshape(bs, ls, d)
    sd = state_ssm_re.dtype
    return (y_prompt, y_sample, re_p[None].astype(sd), im_p[None].astype(sd), s_p[None].astype(state_hgrn.dtype),
            re_s[None].astype(sd), im_s[None].astype(sd), s_s[None].astype(state_hgrn.dtype))
```

```python
import functools
import math

import jax
import jax.numpy as jnp
from jax import lax
from jax.experimental import pallas as pl
from jax.experimental.pallas import tpu as pltpu

F32 = jnp.float32
BF16 = jnp.bfloat16
TOP_K = 4
NORM_EPS = 1e-5
SWIGLU_LIMIT = 7.0
SWIGLU_ALPHA = 1.702
LANES = 128
SUBLANES = 8
FOLD = SUBLANES
HG_CHUNK = 64
HG_SUB = 16
VMEM_LIMIT_BYTES = 56 * 1024 * 1024


def _cparams(n_axes):
    return pltpu.CompilerParams(dimension_semantics=("arbitrary",) * n_axes,
                                vmem_limit_bytes=VMEM_LIMIT_BYTES)


def _pick_tile(n, target, mult):
    best = None
    for t in range(mult, min(n, target) + 1, mult):
        if n % t == 0:
            best = t
    assert best is not None, (n, target, mult)
    return best


def _sigmoid(x):
    return 1.0 / (1.0 + jnp.exp(-x))


def _split3(x):
    h1 = x.astype(BF16)
    r1 = x - h1.astype(F32)
    h2 = r1.astype(BF16)
    r2 = r1 - h2.astype(F32)
    return h1, h2, r2.astype(BF16)


def _cast_rows(src_ref, dst_ref, chunk):
    rows = src_ref.shape[0]
    chunk = min(chunk, rows)
    assert rows % chunk == 0

    def body(r, c):
        sl = pl.ds(pl.multiple_of(r * chunk, chunk), chunk)
        dst_ref[sl, :] = src_ref[sl, :].astype(dst_ref.dtype)
        return c

    lax.fori_loop(0, rows // chunk, body, 0)


def _rmsnorm_kernel(x_ref, g_ref, o_ref):
    x = x_ref[...]
    ms = jnp.mean(x * x, axis=-1, keepdims=True)
    o_ref[...] = (x * lax.rsqrt(ms + NORM_EPS) * g_ref[...]).astype(o_ref.dtype)


def _rmsnorm(x, g, out_dtype):
    t, d = x.shape
    tm = _pick_tile(t, 256, 16)
    return pl.pallas_call(
        _rmsnorm_kernel,
        grid=(t // tm,),
        in_specs=[pl.BlockSpec((tm, d), lambda i: (i, 0)), pl.BlockSpec((1, d), lambda i: (0, 0))],
        out_specs=pl.BlockSpec((tm, d), lambda i: (i, 0)),
        out_shape=jax.ShapeDtypeStruct((t, d), out_dtype),
        compiler_params=_cparams(1),
        name="rmsnorm",
    )(x, g.reshape(1, d))


def _mm_kernel(*refs, n_a, has_res, fold_out):
    a_refs = refs[:n_a]
    w_ref = refs[n_a]
    pos = n_a + 1
    res_ref = None
    if has_res:
        res_ref = refs[pos]
        pos += 1
    o_ref = refs[pos]
    w_bf = refs[pos + 1]

    @pl.when(pl.program_id(1) == 0)
    def _():
        _cast_rows(w_ref, w_bf, 256)

    acc = None
    k0 = 0
    for a_ref in a_refs:
        ka = a_ref.shape[1]
        part = jnp.dot(a_ref[...], w_bf[k0:k0 + ka, :], preferred_element_type=F32)
        acc = part if acc is None else acc + part
        k0 += ka
    if has_res:
        acc = acc + res_ref[...]
    if not fold_out:
        o_ref[...] = acc.astype(o_ref.dtype)
    else:
        slab = refs[pos + 2]
        tm = acc.shape[0]
        for k in range(acc.shape[1] // LANES):
            slab[k] = acc[:, k * LANES:(k + 1) * LANES]
        for k in range(acc.shape[1] // LANES):
            for s in range(FOLD):
                o_ref[k, :, s * LANES:(s + 1) * LANES] = slab[k, pl.ds(s, tm // FOLD, stride=FOLD), :]


def _matmul(a_list, w, col0, ncols, *, res=None, fold_out=False, out_dtype=F32, tm_target=512, tn=512):
    t = a_list[0].shape[0]
    k_total = sum(a.shape[1] for a in a_list)
    assert w.shape[0] == k_total
    tn = math.gcd(math.gcd(ncols, col0), tn)
    assert tn % LANES == 0
    tm = _pick_tile(t, tm_target, SUBLANES * FOLD if fold_out else 16)
    nj, ni = ncols // tn, t // tm
    j0 = col0 // tn
    in_specs = [pl.BlockSpec((tm, a.shape[1]), lambda j, i: (i, 0)) for a in a_list]
    in_specs.append(pl.BlockSpec((k_total, tn), lambda j, i: (0, j + j0)))
    args = list(a_list) + [w]
    if res is not None:
        in_specs.append(pl.BlockSpec((tm, tn), lambda j, i: (i, j)))
        args.append(res)
    scratch = [pltpu.VMEM((k_total, tn), BF16)]
    if fold_out:
        nk = tn // LANES
        out_shape = jax.ShapeDtypeStruct((ncols // LANES, t // FOLD, FOLD * LANES), F32)
        out_spec = pl.BlockSpec((nk, tm // FOLD, FOLD * LANES), lambda j, i: (j, i, 0))
        scratch.append(pltpu.VMEM((nk, tm, LANES), F32))
    else:
        out_shape = jax.ShapeDtypeStruct((t, ncols), out_dtype)
        out_spec = pl.BlockSpec((tm, tn), lambda j, i: (i, j))
    return pl.pallas_call(
        functools.partial(_mm_kernel, n_a=len(a_list), has_res=res is not None, fold_out=fold_out),
        grid=(nj, ni),
        in_specs=in_specs,
        out_specs=out_spec,
        out_shape=out_shape,
        scratch_shapes=scratch,
        compiler_params=_cparams(2),
        name="dense_matmul",
    )(*args)


def _tile_blockdiag(x, rep, col_group):
    r, c = x.shape[-2:]
    xt = jnp.broadcast_to(x[..., None, :, :], x.shape[:-2] + (rep, r, c))
    a_idx = lax.broadcasted_iota(jnp.int32, (rep, r, c), 0)
    c_idx = lax.broadcasted_iota(jnp.int32, (rep, r, c), 2) // col_group
    return jnp.where(a_idx == c_idx, xt, 0.0).reshape(x.shape[:-2] + (rep * r, c))


def _s5_tables(a_re, a_im, log_dt, b_re, b_im, c_re, c_im, d_skip, ls):
    hi = lax.Precision.HIGHEST
    g, p, gc = b_re.shape
    gt = LANES // gc
    nt = g // gt
    cs = FOLD
    dt = jnp.exp(log_dt.astype(F32))[:, None]
    ar, ai = a_re.astype(F32) * dt, a_im.astype(F32) * dt
    mag = jnp.exp(ar)
    lr, li = mag * jnp.cos(ai), mag * jnp.sin(ai)
    den = a_re * a_re + a_im * a_im
    half = jnp.sin(0.5 * ai)
    nr, ni = jnp.expm1(ar) * jnp.cos(ai) - 2.0 * half * half, li
    fr, fi = (nr * a_re + ni * a_im) / den, (ni * a_re - nr * a_im) / den
    bbr = fr[..., None] * b_re - fi[..., None] * b_im
    bbi = fr[..., None] * b_im + fi[..., None] * b_re
    taus = jnp.arange(cs + 1, dtype=F32)[:, None, None]
    pmag = jnp.exp(taus * ar)
    pr, pi = pmag * jnp.cos(taus * ai), pmag * jnp.sin(taus * ai)
    wr = pr[..., None] * bbr - pi[..., None] * bbi
    wi = pr[..., None] * bbi + pi[..., None] * bbr
    kt = (jnp.einsum('gcp,tgpd->tgcd', c_re, wr[:cs], precision=hi)
          - jnp.einsum('gcp,tgpd->tgcd', c_im, wi[:cs], precision=hi))
    def blocks(x, rows, cols):
        x = x.reshape(x.shape[0], nt, gt, cols, rows).transpose(0, 1, 4, 2, 3)
        return _tile_blockdiag(x.reshape(x.shape[0], nt, rows, gt * cols), gt, cols).astype(BF16)

    bd = blocks(kt, gc, gc)
    zero = jnp.zeros_like(bd[0])
    m_loc = jnp.concatenate(
        [jnp.concatenate([zero] * s + [bd[tau] for tau in range(cs - s)], axis=2) for s in range(cs)], axis=1)
    er = blocks(wr[:cs], gc, p)
    ei = blocks(wi[:cs], gc, p)
    m_end = jnp.concatenate(
        [jnp.concatenate([er[cs - 1 - s], ei[cs - 1 - s]], axis=2) for s in range(cs)], axis=1)
    cr = c_re[None] * pr[1:, :, None, :] - c_im[None] * pi[1:, :, None, :]
    ci = c_re[None] * pi[1:, :, None, :] + c_im[None] * pr[1:, :, None, :]
    cbr = blocks(cr, p, gc)
    cbi = blocks(-ci, p, gc)
    m_car = jnp.concatenate([jnp.concatenate([cbr[t] for t in range(cs)], axis=2),
                             jnp.concatenate([cbi[t] for t in range(cs)], axis=2)], axis=1)
    dvec = jnp.broadcast_to(d_skip.astype(F32).reshape(nt, 1, 1, gt * gc), (nt, 1, cs, gt * gc)).reshape(nt, 1, cs * gt * gc)
    lam = lambda n: (pr[n].reshape(nt, 1, gt * p), pi[n].reshape(nt, 1, gt * p))
    return (m_loc, m_end, m_car, dvec) + lam(cs) + lam(ls)


def _gelu_tanh(x):
    return 0.5 * x * (1.0 + jnp.tanh(math.sqrt(2.0 / math.pi) * (x + 0.044715 * (x * x * x))))


def _s5_seq_kernel(u_ref, ml_ref, me_ref, mc_ref, dv_ref, lr_ref, li_ref, z_ref, st_ref, el_scr, hin_scr):
    u = u_ref[0]
    ub = u.astype(BF16)
    rows = u.shape[0]
    hw = lr_ref.shape[2]
    y = jnp.dot(ub, ml_ref[0], preferred_element_type=F32)
    el_scr[...] = jnp.dot(ub, me_ref[0], preferred_element_type=F32)
    lam_r, lam_i = lr_ref[0], li_ref[0]

    def body(n, carry):
        hr, hi_ = carry
        hin_scr[pl.ds(n, 1), 0:hw] = hr
        hin_scr[pl.ds(n, 1), hw:2 * hw] = hi_
        e = el_scr[pl.ds(n, 1), :]
        return (lam_r * hr - lam_i * hi_ + e[:, 0:hw], lam_r * hi_ + lam_i * hr + e[:, hw:2 * hw])

    zero = jnp.zeros((1, hw), F32)
    hr, hi_ = lax.fori_loop(0, rows, body, (zero, zero))
    st_ref[0, 0, :, 0:hw] = hr
    st_ref[0, 0, :, hw:2 * hw] = hi_
    y = y + jnp.dot(hin_scr[...].astype(BF16), mc_ref[0], preferred_element_type=F32) + dv_ref[0] * u
    z_ref[0] = _gelu_tanh(y)


def _s5_batch_kernel(zprev_ref, u_ref, h0_ref, ml_ref, me_ref, mc_ref, dv_ref, lr_ref, li_ref, z_ref, st_ref, *, nseq):
    del zprev_ref
    hw = lr_ref.shape[2]
    wq = ml_ref.shape[1]
    lam_r, lam_i = lr_ref[0], li_ref[0]
    for q in range(nseq):
        u = u_ref[0, :, q * wq:(q + 1) * wq]
        ub = u.astype(BF16)
        o = 2 * hw * q
        h0 = h0_ref[0, :, o:o + 2 * hw]
        y = jnp.dot(ub, ml_ref[0], preferred_element_type=F32)
        y = y + jnp.dot(h0.astype(BF16), mc_ref[0], preferred_element_type=F32) + dv_ref[0] * u
        z_ref[0, :, q * wq:(q + 1) * wq] = _gelu_tanh(y)
        el = jnp.dot(ub, me_ref[0], preferred_element_type=F32)
        hr, hi_ = h0[:, 0:hw], h0[:, hw:2 * hw]
        st_ref[0, :, o:o + hw] = lam_r * hr - lam_i * hi_ + el[:, 0:hw]
        st_ref[0, :, o + hw:o + 2 * hw] = lam_r * hi_ + lam_i * hr + el[:, hw:2 * hw]


def _s5_mixer(u_fold, ssm, h0_re, h0_im, bp, lp, bs, ls):
    a_re, a_im, log_dt, b_re, b_im, c_re, c_im, d_skip = ssm
    g, p, gc = b_re.shape
    gt = LANES // gc
    nt, tf, wf = u_fold.shape
    assert g % gt == 0 and nt == g // gt and wf == FOLD * LANES
    assert lp % FOLD == 0 and FOLD % ls == 0
    nseq = FOLD // ls
    assert bs % nseq == 0
    rp = lp // FOLD
    rs = bs // nseq
    tp_f = bp * rp
    assert tp_f % rs == 0 and tf == tp_f + rs
    hw = gt * p

    m_loc, m_end, m_car, dvec, lr_c, li_c, lr_s, li_s = _s5_tables(
        a_re, a_im, log_dt, b_re, b_im, c_re, c_im, d_skip, ls)
    tabs = (m_loc, m_end, m_car, dvec, lr_c, li_c)
    tile = lambda shape: pl.BlockSpec((1,) + shape, lambda j, b: (j, 0, 0))
    z_p, st_p = pl.pallas_call(
        _s5_seq_kernel,
        grid=(nt, bp),
        in_specs=[pl.BlockSpec((1, rp, wf), lambda j, b: (j, b, 0)),
                  tile((wf, wf)), tile((wf, 2 * hw)), tile((2 * hw, wf)),
                  tile((1, wf)), tile((1, hw)), tile((1, hw))],
        out_specs=[pl.BlockSpec((1, rp, wf), lambda j, b: (j, b, 0)),
                   pl.BlockSpec((1, 1, 1, 2 * hw), lambda j, b: (j, b, 0, 0))],
        out_shape=[jax.ShapeDtypeStruct((nt, tf, wf), F32),
                   jax.ShapeDtypeStruct((nt, bp, 1, 2 * hw), F32)],
        scratch_shapes=[pltpu.VMEM((rp, 2 * hw), F32), pltpu.VMEM((rp, 2 * hw), F32)],
        compiler_params=_cparams(2),
        name="s5_prompt",
    )(u_fold, *tabs)

    h0 = jnp.stack([h0_re.astype(F32), h0_im.astype(F32)], axis=1)
    h0 = h0.reshape(rs, nseq, 2, nt, gt * p).transpose(3, 0, 1, 2, 4).reshape(nt, rs, nseq * 2 * hw)
    sblk = tp_f // rs
    tile1 = lambda shape: pl.BlockSpec((1,) + shape, lambda j: (j, 0, 0))
    wq = ls * LANES
    z_all, st_s = pl.pallas_call(
        functools.partial(_s5_batch_kernel, nseq=nseq),
        grid=(nt,),
        in_specs=[pl.BlockSpec(memory_space=pl.ANY),
                  pl.BlockSpec((1, rs, wf), lambda j: (j, sblk, 0)),
                  tile1((rs, nseq * 2 * hw)),
                  tile1((wq, wq)),
                  pl.BlockSpec((1, wq, 2 * hw), lambda j: (j, nseq - 1, 0)),
                  tile1((2 * hw, wq)),
                  tile1((1, wq)), tile1((1, hw)), tile1((1, hw))],
        out_specs=[pl.BlockSpec((1, rs, wf), lambda j: (j, sblk, 0)),
                   tile1((rs, nseq * 2 * hw))],
        out_shape=[jax.ShapeDtypeStruct((nt, tf, wf), F32),
                   jax.ShapeDtypeStruct((nt, rs, nseq * 2 * hw), F32)],
        input_output_aliases={0: 0},
        compiler_params=_cparams(1),
        name="s5_sample",
    )(z_p, u_fold, h0, m_loc, m_end, m_car, dvec, lr_s, li_s)

    st_p = st_p.reshape(nt, bp, 2, gt, p).transpose(2, 1, 0, 3, 4).reshape(2, bp, g, p)
    st_s = st_s.reshape(nt, rs, nseq, 2, gt, p).transpose(3, 1, 2, 0, 4, 5).reshape(2, bs, g, p)
    return z_all, st_p[0], st_p[1], st_s[0], st_s[1]


def _glu_kernel(zf_ref, w_ref, b_ref, o_ref, w_bf, z_nat):
    j = pl.program_id(0)

    @pl.when(pl.program_id(1) == 0)
    def _():
        _cast_rows(w_ref, w_bf, 256)

    nt, rf, _ = zf_ref.shape
    tm = rf * FOLD
    for k in range(nt):
        for t in range(FOLD):
            z_nat[k, pl.ds(t, rf, stride=FOLD), :] = zf_ref[k, :, t * LANES:(t + 1) * LANES]
    z = jnp.concatenate([z_nat[k] for k in range(nt)], axis=-1)
    tn = o_ref.shape[1]
    acc = jnp.dot(z.astype(BF16), w_bf[...], preferred_element_type=F32) + b_ref[...]
    nk = tn // LANES
    for kk in range(nk):
        zt = z_nat[j * nk + kk]
        o_ref[:, kk * LANES:(kk + 1) * LANES] = (zt * _sigmoid(acc[:, kk * LANES:(kk + 1) * LANES])).astype(o_ref.dtype)
    del tm


def _glu(z_fold, w_glu, b_glu):
    nt, tf, wf = z_fold.shape
    t = tf * FOLD
    n = nt * LANES
    tn = min(512, n)
    tm = _pick_tile(t, 512, SUBLANES * FOLD)
    return pl.pallas_call(
        _glu_kernel,
        grid=(n // tn, t // tm),
        in_specs=[pl.BlockSpec((nt, tm // FOLD, wf), lambda j, i: (0, i, 0)),
                  pl.BlockSpec((n, tn), lambda j, i: (0, j)),
                  pl.BlockSpec((1, tn), lambda j, i: (0, j))],
        out_specs=pl.BlockSpec((tm, tn), lambda j, i: (i, j)),
        out_shape=jax.ShapeDtypeStruct((t, n), BF16),
        scratch_shapes=[pltpu.VMEM((n, tn), BF16), pltpu.VMEM((nt, tm, LANES), F32)],
        compiler_params=_cparams(2),
        name="s5_glu",
    )(z_fold, w_glu, b_glu.reshape(1, n))


def _hg_gates(qin, fin, lb):
    q = qin * _sigmoid(qin)
    fg = lb + (1.0 - lb) * _sigmoid(fin)
    return q, fg, jnp.log(fg), 1.0 - fg


def _cumsum_rows(logf, tri_bf):
    h1, h2, h3 = _split3(logf)
    return (jnp.dot(tri_bf, h1, preferred_element_type=F32) + jnp.dot(tri_bf, h2, preferred_element_type=F32)
            + jnp.dot(tri_bf, h3, preferred_element_type=F32))


def _hg_out(o, gin, gain):
    o = o * lax.rsqrt(jnp.mean(o * o, axis=-1, keepdims=True) + NORM_EPS) * gain
    return o * (gin * _sigmoid(gin))


def _hg_prompt_kernel(q_ref, f_ref, i_ref, g_ref, lb_ref, gain_ref, o_ref, s_ref, st):
    n = pl.program_id(2)
    c, sub = HG_CHUNK, HG_SUB
    nsub = c // sub

    @pl.when(n == 0)
    def _():
        st[...] = jnp.zeros_like(st)

    lb = lb_ref[0]
    gain = gain_ref[...]
    ri = lax.broadcasted_iota(jnp.int32, (c, c), 0)
    ci = lax.broadcasted_iota(jnp.int32, (c, c), 1)
    tri = (ri >= ci).astype(BF16)
    t_io = lax.broadcasted_iota(jnp.int32, (sub, LANES), 0)
    nt_dims = (((1,), (1,)), ((), ()))
    tn_dims = (((0,), (0,)), ((), ()))

    for cc in range(q_ref.shape[0] // c):
        rows = slice(cc * c, (cc + 1) * c)
        q, fg, logf, k = _hg_gates(q_ref[rows, :], f_ref[rows, :], lb)
        v = i_ref[rows, :]
        b = _cumsum_rows(logf, tri)
        vb = v.astype(BF16)
        s_t = st[...]
        o_parts = []
        for i in range(nsub):
            sl = slice(i * sub, (i + 1) * sub)
            b_i, q_i, k_i, v_i = b[sl], q[sl], k[sl], v[sl]
            r_i = b[i * sub - 1:i * sub] if i > 0 else jnp.zeros((1, LANES), F32)
            o_i = jnp.zeros((sub, LANES), F32)
            for s in range(sub):
                arg = jnp.where(t_io >= s, b_i - b_i[s:s + 1], -jnp.inf)
                w = q_i * k_i[s:s + 1] * jnp.exp(arg)
                col = jnp.sum(w, axis=-1, keepdims=True)
                o_i = o_i + col * v_i[s:s + 1]
            if i > 0:
                qt = (q_i * jnp.exp(b_i - r_i)).astype(BF16)
                kfull = (k[0:i * sub] * jnp.exp(r_i - b[0:i * sub])).astype(BF16)
                a = lax.dot_general(qt, kfull, nt_dims, preferred_element_type=F32)
                o_i = o_i + jnp.dot(a.astype(BF16), vb[0:i * sub], preferred_element_type=F32)
            o_parts.append(o_i)
        o = jnp.concatenate(o_parts, axis=0)
        qe = (q * jnp.exp(b)).astype(BF16)
        o = o + lax.dot_general(qe, s_t.astype(BF16), nt_dims, preferred_element_type=F32)
        b_last = b[c - 1:c]
        khat = (k * jnp.exp(b_last - b)).astype(BF16)
        st[...] = s_t * jnp.exp(b_last) + lax.dot_general(vb, khat, tn_dims, preferred_element_type=F32)
        o_ref[rows, :] = _hg_out(o, g_ref[rows, :], gain).astype(o_ref.dtype)

    @pl.when(n == pl.num_programs(2) - 1)
    def _():
        s_ref[0, 0] = st[...].T


def _hg_sample_kernel(hprev_ref, q_ref, f_ref, i_ref, g_ref, lb_ref, gain_ref, s0_ref, o_ref, s_ref, *, ls):
    del hprev_ref
    nseq = s0_ref.shape[0]
    rows = nseq * ls
    lb = lb_ref[0]
    q, fg, logf, k = _hg_gates(q_ref[...], f_ref[...], lb)
    v = i_ref[...]
    ri = lax.broadcasted_iota(jnp.int32, (rows, rows), 0)
    ci = lax.broadcasted_iota(jnp.int32, (rows, rows), 1)
    tri = ((ri >= ci) & (ri // ls == ci // ls)).astype(BF16)
    b = _cumsum_rows(logf, tri)
    step = lax.broadcasted_iota(jnp.int32, (rows, LANES), 0) % ls
    o = jnp.zeros((rows, LANES), F32)
    for d in range(ls):
        if d == 0:
            ks, bs_, vs = k, b, v
        else:
            ks, bs_, vs = pltpu.roll(k, d, 0), pltpu.roll(b, d, 0), pltpu.roll(v, d, 0)
        arg = jnp.where(step >= d, b - bs_, -jnp.inf)
        col = jnp.sum(q * ks * jnp.exp(arg), axis=-1, keepdims=True)
        o = o + col * vs
    qe = (q * jnp.exp(b)).astype(BF16)
    tn_dims = (((0,), (0,)), ((), ()))
    last = ((ci == (ri // ls) * ls + ls - 1)).astype(BF16)
    b_end = _cumsum_rows(b, last)
    khat = k * jnp.exp(b_end - b)
    e_end = jnp.exp(b_end)
    seq_of_row = lax.broadcasted_iota(jnp.int32, (rows, LANES), 0) // ls
    erow = lax.broadcasted_iota(jnp.int32, (2 * SUBLANES, LANES), 0)
    ones = jnp.ones((2 * SUBLANES, LANES), BF16)
    vb = v.astype(BF16)
    for r in range(nseq):
        mine = seq_of_row == r
        s0 = s0_ref[r, 0]
        o = o + jnp.where(mine, jnp.dot(qe, s0.astype(BF16), preferred_element_type=F32), 0.0)
        kv = lax.dot_general(jnp.where(mine, khat, 0.0).astype(BF16), vb, tn_dims, preferred_element_type=F32)
        e_r = e_end[r * ls:r * ls + 1]
        e_hi = e_r.astype(BF16).astype(F32)
        emat = jnp.where(erow == 0, e_hi, jnp.where(erow == 1, e_r - e_hi, 0.0)).astype(BF16)
        decay = lax.dot_general(emat, ones, tn_dims, preferred_element_type=F32)
        s_ref[r, 0] = s0 * decay + kv
    o_ref[...] = _hg_out(o, g_ref[...], gain_ref[...]).astype(o_ref.dtype)


def _hgrn_mixer(qfig, lb, gain, s0, bp, lp, bs, ls):
    t = qfig.shape[0]
    h, dk = lb.shape
    assert dk == LANES and qfig.shape[1] == 4 * h * dk
    rb = _pick_tile(lp, 256, HG_CHUNK)
    nb = lp // rb
    lb3 = lb.reshape(h, 1, dk)
    gain2 = gain.reshape(1, dk)
    col = lambda off: pl.BlockSpec((rb, dk), lambda b, hh, n: (b * nb + n, off * h + hh))
    out_p, s_p = pl.pallas_call(
        _hg_prompt_kernel,
        grid=(bp, h, nb),
        in_specs=[col(0), col(1), col(2), col(3),
                  pl.BlockSpec((1, 1, dk), lambda b, hh, n: (hh, 0, 0)),
                  pl.BlockSpec((1, dk), lambda b, hh, n: (0, 0))],
        out_specs=[pl.BlockSpec((rb, dk), lambda b, hh, n: (b * nb + n, hh)),
                   pl.BlockSpec((1, 1, dk, dk), lambda b, hh, n: (b, hh, 0, 0))],
        out_shape=[jax.ShapeDtypeStruct((t, h * dk), BF16),
                   jax.ShapeDtypeStruct((bp, h, dk, dk), F32)],
        scratch_shapes=[pltpu.VMEM((dk, dk), F32)],
        compiler_params=_cparams(3),
        name="hgrn_prompt",
    )(qfig, qfig, qfig, qfig, lb3, gain2)

    sb = _pick_tile(bs, 8, 1)
    rows = sb * ls
    assert rows % 16 == 0 and (bp * lp) % rows == 0
    r0 = (bp * lp) // rows
    scol = lambda off: pl.BlockSpec((rows, dk), lambda i, hh: (r0 + i, off * h + hh))
    out_all, s_s = pl.pallas_call(
        functools.partial(_hg_sample_kernel, ls=ls),
        grid=(bs // sb, h),
        in_specs=[pl.BlockSpec(memory_space=pl.ANY),
                  scol(0), scol(1), scol(2), scol(3),
                  pl.BlockSpec((1, 1, dk), lambda i, hh: (hh, 0, 0)),
                  pl.BlockSpec((1, dk), lambda i, hh: (0, 0)),
                  pl.BlockSpec((sb, 1, dk, dk), lambda i, hh: (i, hh, 0, 0))],
        out_specs=[pl.BlockSpec((rows, dk), lambda i, hh: (r0 + i, hh)),
                   pl.BlockSpec((sb, 1, dk, dk), lambda i, hh: (i, hh, 0, 0))],
        out_shape=[jax.ShapeDtypeStruct((t, h * dk), BF16),
                   jax.ShapeDtypeStruct((bs, h, dk, dk), F32)],
        input_output_aliases={0: 0},
        compiler_params=_cparams(2),
        name="hgrn_sample",
    )(out_p, qfig, qfig, qfig, qfig, lb3, gain2, s0)
    return out_all, s_p, s_s


def _router_kernel(x_ref, g_ref, w_ref, b_ref, t_ref, idx_ref, gate_ref):
    x = x_ref[...]
    ms = jnp.mean(x * x, axis=-1, keepdims=True)
    tok = x * lax.rsqrt(ms + NORM_EPS) * g_ref[...]
    t_ref[...] = tok
    t_hi = tok.astype(BF16)
    t_lo = (tok - t_hi.astype(F32)).astype(BF16)
    w = w_ref[...]
    w_hi = w.astype(BF16)
    w_lo = (w - w_hi.astype(F32)).astype(BF16)
    logits = (jnp.dot(t_hi, w_hi, preferred_element_type=F32) + jnp.dot(t_hi, w_lo, preferred_element_type=F32)
              + jnp.dot(t_lo, w_hi, preferred_element_type=F32)) + b_ref[...]
    ne = logits.shape[1]
    lane = lax.broadcasted_iota(jnp.int32, logits.shape, 1).astype(F32)
    out_lane = lax.broadcasted_iota(jnp.int32, idx_ref.shape, 1)
    idx_out = jnp.zeros(idx_ref.shape, jnp.int32)
    val_out = jnp.zeros(idx_ref.shape, F32)
    vals = []
    cur = logits
    for r in range(TOP_K):
        m = jnp.max(cur, axis=-1, keepdims=True)
        sel = jnp.min(jnp.where(cur == m, lane, float(ne)), axis=-1, keepdims=True)
        cur = jnp.where(lane == sel, -jnp.inf, cur)
        vals.append(m)
        idx_out = jnp.where(out_lane == r, sel.astype(jnp.int32), idx_out)
    denom = sum(jnp.exp(v - vals[0]) for v in vals)
    for r in range(TOP_K):
        val_out = jnp.where(out_lane == r, jnp.exp(vals[r] - vals[0]) / denom, val_out)
    idx_ref[...] = idx_out
    gate_ref[...] = val_out


def _router(x1, g_ffn, w_router, b_router):
    t, d = x1.shape
    ne = w_router.shape[1]
    tm = _pick_tile(t, 256, 16)
    return pl.pallas_call(
        _router_kernel,
        grid=(t // tm,),
        in_specs=[pl.BlockSpec((tm, d), lambda i: (i, 0)), pl.BlockSpec((1, d), lambda i: (0, 0)),
                  pl.BlockSpec((d, ne), lambda i: (0, 0)), pl.BlockSpec((1, ne), lambda i: (0, 0))],
        out_specs=[pl.BlockSpec((tm, d), lambda i: (i, 0)), pl.BlockSpec((tm, LANES), lambda i: (i, 0)),
                   pl.BlockSpec((tm, LANES), lambda i: (i, 0))],
        out_shape=[jax.ShapeDtypeStruct((t, d), F32), jax.ShapeDtypeStruct((t, LANES), jnp.int32),
                   jax.ShapeDtypeStruct((t, LANES), F32)],
        compiler_params=_cparams(1),
        name="router",
    )(x1, g_ffn.reshape(1, d), w_router, b_router.reshape(1, ne))


def _row_copy(src_hbm, dst_vmem, src_row, dst_row, sem):
    return pltpu.make_async_copy(src_hbm.at[pl.ds(src_row, 1), :], dst_vmem.at[pl.ds(dst_row, 1), :], sem)


def _gather_kernel(tok_tab, src_hbm, o_ref, buf, sem):
    bm = buf.shape[1]
    i = pl.program_id(0)

    def issue(step, slot):
        def start(r, c):
            _row_copy(src_hbm, buf.at[slot], tok_tab[step * bm + r], r, sem.at[slot]).start()
            return c
        lax.fori_loop(0, bm, start, 0, unroll=8)

    @pl.when(i == 0)
    def _():
        issue(0, 0)

    @pl.when(i + 1 < pl.num_programs(0))
    def _():
        issue(i + 1, (i + 1) % 2)

    slot = i % 2

    def wait(r, c):
        _row_copy(src_hbm, buf.at[slot], 0, r, sem.at[slot]).wait()
        return c

    lax.fori_loop(0, bm, wait, 0, unroll=8)
    o_ref[...] = buf[slot].astype(o_ref.dtype)


def _gather_rows(src, row_tok, bm):
    p = row_tok.shape[0]
    d = src.shape[1]
    return pl.pallas_call(
        _gather_kernel,
        grid_spec=pltpu.PrefetchScalarGridSpec(
            num_scalar_prefetch=1,
            grid=(p // bm,),
            in_specs=[pl.BlockSpec(memory_space=pl.ANY)],
            out_specs=pl.BlockSpec((bm, d), lambda i, tab: (i, 0)),
            scratch_shapes=[pltpu.VMEM((2, bm, d), src.dtype), pltpu.SemaphoreType.DMA((2,))],
        ),
        out_shape=jax.ShapeDtypeStruct((p, d), BF16),
        compiler_params=_cparams(1),
        name="moe_gather",
    )(row_tok, src)


def _moe_up_kernel(e_tab, j_tab, rb_tab, first_tab, valid_tab, x_ref, wg_ref, wu_ref, bg_ref, bu_ref, o_ref, wg_bf, wu_bf):
    s = pl.program_id(0)

    @pl.when(first_tab[s] == 1)
    def _():
        _cast_rows(wg_ref, wg_bf, 256)
        _cast_rows(wu_ref, wu_bf, 256)

    @pl.when(valid_tab[s] == 1)
    def _():
        x = x_ref[...]
        gate = jnp.dot(x, wg_bf[...], preferred_element_type=F32) + bg_ref[...]
        up = jnp.dot(x, wu_bf[...], preferred_element_type=F32) + bu_ref[...]
        gate = jnp.minimum(gate, SWIGLU_LIMIT)
        up = jnp.clip(up, -SWIGLU_LIMIT, SWIGLU_LIMIT)
        hid = (up + 1.0) * gate * _sigmoid(SWIGLU_ALPHA * gate)
        o_ref[...] = hid.astype(o_ref.dtype)


def _moe_down_kernel(e_tab, j_tab, rb_tab, first_tab, valid_tab, h_ref, wd_ref, bd_ref, o_ref, wd_bf):
    s = pl.program_id(0)

    @pl.when(first_tab[s] == 1)
    def _():
        _cast_rows(wd_ref, wd_bf, 256)

    @pl.when(valid_tab[s] == 1)
    def _():
        o_ref[...] = jnp.dot(h_ref[...], wd_bf[...], preferred_element_type=F32) + bd_ref[...]


def _moe_tables(nb_e, nj, n_steps):
    ne = nb_e.shape[0]
    blk_end = jnp.cumsum(nb_e)
    blk_start = blk_end - nb_e
    total = blk_end[-1]
    s = jnp.minimum(jnp.arange(n_steps, dtype=jnp.int32), nj * total - 1)
    e_s = jnp.minimum(jnp.sum((s[:, None] >= nj * blk_end[None, :]).astype(jnp.int32), axis=1), ne - 1)
    local = s - nj * blk_start[e_s]
    nbe = jnp.maximum(nb_e[e_s], 1)
    j_s = (local // nbe).astype(jnp.int32)
    rb_s = (blk_start[e_s] + local % nbe).astype(jnp.int32)
    valid = (jnp.arange(n_steps) < nj * total).astype(jnp.int32)
    prev_e = jnp.concatenate([jnp.full((1,), -1, jnp.int32), e_s[:-1]])
    prev_j = jnp.concatenate([jnp.full((1,), -1, jnp.int32), j_s[:-1]])
    first = (((e_s != prev_e) | (j_s != prev_j)) & (valid == 1)).astype(jnp.int32)
    return e_s, j_s, rb_s, first, valid


def _moe_experts(xs, nb_e, w_gate, b_gate, w_up, b_up, w_down, b_down, bm):
    p, d = xs.shape
    ne, _, f = w_gate.shape
    tn = min(512, f)
    nb = p // bm
    nj = f // tn
    tabs = _moe_tables(nb_e, nj, nj * nb)
    wspec = lambda kdim: pl.BlockSpec((None, kdim, tn), lambda s, e, j, rb, fi, va: (e[s], 0, j[s]))
    bspec = pl.BlockSpec((None, 1, tn), lambda s, e, j, rb, fi, va: (e[s], 0, j[s]))
    hid = pl.pallas_call(
        _moe_up_kernel,
        grid_spec=pltpu.PrefetchScalarGridSpec(
            num_scalar_prefetch=5,
            grid=(nj * nb,),
            in_specs=[pl.BlockSpec((bm, d), lambda s, e, j, rb, fi, va: (rb[s], 0)),
                      wspec(d), wspec(d), bspec, bspec],
            out_specs=pl.BlockSpec((bm, tn), lambda s, e, j, rb, fi, va: (rb[s], j[s])),
            scratch_shapes=[pltpu.VMEM((d, tn), BF16), pltpu.VMEM((d, tn), BF16)],
        ),
        out_shape=jax.ShapeDtypeStruct((p, f), BF16),
        compiler_params=_cparams(1),
        name="moe_up",
    )(*tabs, xs, w_gate, w_up, b_gate.reshape(ne, 1, f), b_up.reshape(ne, 1, f))
    tn2 = min(512, d)
    nj2 = d // tn2
    tabs2 = _moe_tables(nb_e, nj2, nj2 * nb)
    y = pl.pallas_call(
        _moe_down_kernel,
        grid_spec=pltpu.PrefetchScalarGridSpec(
            num_scalar_prefetch=5,
            grid=(nj2 * nb,),
            in_specs=[pl.BlockSpec((bm, f), lambda s, e, j, rb, fi, va: (rb[s], 0)),
                      pl.BlockSpec((None, f, tn2), lambda s, e, j, rb, fi, va: (e[s], 0, j[s])),
                      pl.BlockSpec((None, 1, tn2), lambda s, e, j, rb, fi, va: (e[s], 0, j[s]))],
            out_specs=pl.BlockSpec((bm, tn2), lambda s, e, j, rb, fi, va: (rb[s], j[s])),
            scratch_shapes=[pltpu.VMEM((f, tn2), BF16)],
        ),
        out_shape=jax.ShapeDtypeStruct((p, d), F32),
        compiler_params=_cparams(1),
        name="moe_down",
    )(*tabs2, hid, w_down, b_down.reshape(ne, 1, d))
    return y


def _combine_kernel(pos_tab, y_hbm, x_ref, gate_ref, g_ref, o_ref, buf, sem):
    tc = x_ref.shape[0]
    i = pl.program_id(0)

    def issue(step, slot):
        def start(r, c):
            for k in range(TOP_K):
                _row_copy(y_hbm, buf.at[slot, k], pos_tab[(step * tc + r) * TOP_K + k], r, sem.at[slot]).start()
            return c
        lax.fori_loop(0, tc, start, 0, unroll=2)

    @pl.when(i == 0)
    def _():
        issue(0, 0)

    @pl.when(i + 1 < pl.num_programs(0))
    def _():
        issue(i + 1, (i + 1) % 2)

    slot = i % 2

    def wait(r, c):
        for k in range(TOP_K):
            _row_copy(y_hbm, buf.at[slot, k], 0, r, sem.at[slot]).wait()
        return c

    lax.fori_loop(0, tc, wait, 0, unroll=2)
    gates = gate_ref[...]
    acc = x_ref[...]
    for k in range(TOP_K):
        acc = acc + gates[:, k:k + 1] * buf[slot, k]
    ms = jnp.mean(acc * acc, axis=-1, keepdims=True)
    o_ref[...] = acc * lax.rsqrt(ms + NORM_EPS) * g_ref[...]


def _combine(y, pos, x1, gates, g_final):
    t, d = x1.shape
    tc = _pick_tile(t, 64, 8)
    return pl.pallas_call(
        _combine_kernel,
        grid_spec=pltpu.PrefetchScalarGridSpec(
            num_scalar_prefetch=1,
            grid=(t // tc,),
            in_specs=[pl.BlockSpec(memory_space=pl.ANY),
                      pl.BlockSpec((tc, d), lambda i, tab: (i, 0)),
                      pl.BlockSpec((tc, LANES), lambda i, tab: (i, 0)),
                      pl.BlockSpec((1, d), lambda i, tab: (0, 0))],
            out_specs=pl.BlockSpec((tc, d), lambda i, tab: (i, 0)),
            scratch_shapes=[pltpu.VMEM((2, TOP_K, tc, d), F32), pltpu.SemaphoreType.DMA((2,))],
        ),
        out_shape=jax.ShapeDtypeStruct((t, d), F32),
        compiler_params=_cparams(1),
        name="moe_combine",
    )(pos, y, x1, gates, g_final.reshape(1, d))


def kernel(x_prompt, x_sample, state_ssm_re, state_ssm_im, state_hgrn, g_mix, w_in, ssm_a_re, ssm_a_im, ssm_log_dt, ssm_b_re, ssm_b_im, ssm_c_re, ssm_c_im, ssm_d, w_glu, b_glu, hg_lb_logits, hg_o_gain, w_out, g_ffn, w_router, b_router, w_gate, b_gate, w_up, b_up, w_down, b_down, g_final):
    bp, lp, d = x_prompt.shape
    bs, ls, _ = x_sample.shape
    depth = w_in.shape[0]
    assert depth == 1
    tp, ts = bp * lp, bs * ls
    t = tp + ts
    s5w = ssm_b_re.shape[1] * ssm_b_re.shape[3]
    h, dk = hg_lb_logits.shape[1:]
    ne = w_router.shape[2]

    x0 = jnp.concatenate([x_prompt.reshape(tp, d), x_sample.reshape(ts, d)], axis=0)
    lower_bounds = jnp.cumsum(jax.nn.softmax(hg_lb_logits.astype(F32), axis=0), axis=0)

    hn = _rmsnorm(x0, g_mix[0], BF16)
    u_fold = _matmul([hn], w_in[0], 0, s5w, fold_out=True)
    qfig = _matmul([hn], w_in[0], s5w, w_in.shape[2] - s5w)
    ssm = (ssm_a_re[0], ssm_a_im[0], ssm_log_dt[0], ssm_b_re[0], ssm_b_im[0], ssm_c_re[0], ssm_c_im[0], ssm_d[0])
    z_fold, re_p, im_p, re_s, im_s = _s5_mixer(u_fold, ssm, state_ssm_re[0], state_ssm_im[0], bp, lp, bs, ls)
    s5_out = _glu(z_fold, w_glu[0], b_glu[0])
    hg_out, s_p, s_s = _hgrn_mixer(qfig, lower_bounds[0], hg_o_gain[0], state_hgrn[0], bp, lp, bs, ls)
    x1 = _matmul([s5_out, hg_out], w_out[0], 0, d, res=x0)

    tok, idx, gates = _router(x1, g_ffn[0], w_router[0], b_router[0])
    m = t * TOP_K
    bm = 256
    flat_e = idx[:, :TOP_K].reshape(m)
    onehot = (flat_e[:, None] == jnp.arange(ne, dtype=jnp.int32)[None, :]).astype(jnp.int32)
    seen = jnp.cumsum(onehot, axis=0)
    counts = seen[-1]
    nb_e = (counts + bm - 1) // bm
    pad_end = jnp.cumsum(nb_e * bm)
    pad_start = pad_end - nb_e * bm
    pos = jnp.sum(onehot * (seen - 1 + pad_start[None, :]), axis=1).astype(jnp.int32)
    n_blocks = -(-(m + ne * (bm - 1)) // bm)
    p_rows = n_blocks * bm
    row_tok = jnp.zeros((p_rows,), jnp.int32).at[pos].set(jnp.arange(m, dtype=jnp.int32) // TOP_K)
    xs = _gather_rows(tok, row_tok, bm)
    y_rows = _moe_experts(xs, nb_e, w_gate[0], b_gate[0], w_up[0], b_up[0], w_down[0], b_down[0], bm)
    y_all = _combine(y_rows, pos, x1, gates, g_final)

    y_prompt = y_all[:tp].reshape(bp, lp, d)
    y_sample = y_all[tp:].reshape(bs, ls, d)
    sd = state_ssm_re.dtype
    return (y_prompt, y_sample, re_p[None].astype(sd), im_p[None].astype(sd), s_p[None].astype(state_hgrn.dtype),
            re_s[None].astype(sd), im_s[None].astype(sd), s_s[None].astype(state_hgrn.dtype))
```

```python
import functools
import math

import jax
import jax.numpy as jnp
from jax import lax
from jax.experimental import pallas as pl
from jax.experimental.pallas import tpu as pltpu

F32 = jnp.float32
BF16 = jnp.bfloat16
TOP_K = 4
NORM_EPS = 1e-5
SWIGLU_LIMIT = 7.0
SWIGLU_ALPHA = 1.702
LANES = 128
SUBLANES = 8
FOLD = SUBLANES
HG_CHUNK = 64
HG_SUB = 16
MOE_BLOCK_ROWS = 256
MOE_GROUP_BLOCKS = 8
VMEM_LIMIT_BYTES = 56 * 1024 * 1024


def _cparams(n_axes):
    return pltpu.CompilerParams(dimension_semantics=("arbitrary",) * n_axes,
                                vmem_limit_bytes=VMEM_LIMIT_BYTES)


def _pick_tile(n, target, mult):
    best = None
    for t in range(mult, min(n, target) + 1, mult):
        if n % t == 0:
            best = t
    assert best is not None, (n, target, mult)
    return best


def _sigmoid(x):
    return 1.0 / (1.0 + jnp.exp(-x))


def _split3(x):
    h1 = x.astype(BF16)
    r1 = x - h1.astype(F32)
    h2 = r1.astype(BF16)
    r2 = r1 - h2.astype(F32)
    return h1, h2, r2.astype(BF16)


def _cast_rows(src_ref, dst_ref, chunk):
    rows = src_ref.shape[0]
    chunk = min(chunk, rows)
    assert rows % chunk == 0

    def body(r, c):
        sl = pl.ds(pl.multiple_of(r * chunk, chunk), chunk)
        dst_ref[sl, :] = src_ref[sl, :].astype(dst_ref.dtype)
        return c

    lax.fori_loop(0, rows // chunk, body, 0)


def _split_rows_specs(xp, xs, tm, width, row_axis, col_index):
    npb = xp.shape[0] // tm
    assert xp.shape[0] % tm == 0 and xs.shape[0] % tm == 0
    spec_p = pl.BlockSpec((tm, width), lambda *ids: (jnp.minimum(ids[row_axis], npb - 1), col_index(*ids)))
    spec_s = pl.BlockSpec((tm, width), lambda *ids: (jnp.maximum(ids[row_axis] - npb, 0), col_index(*ids)))
    return [spec_p, spec_s], npb


def _rmsnorm_kernel(xp_ref, xs_ref, g_ref, o_ref, *, npb):
    x = jnp.where(pl.program_id(0) < npb, xp_ref[...], xs_ref[...])
    ms = jnp.mean(x * x, axis=-1, keepdims=True)
    o_ref[...] = (x * lax.rsqrt(ms + NORM_EPS) * g_ref[...]).astype(o_ref.dtype)


def _rmsnorm(xp, xs, g, out_dtype):
    d = xp.shape[1]
    t = xp.shape[0] + xs.shape[0]
    tm = _pick_tile(math.gcd(xp.shape[0], xs.shape[0]), 256, 16)
    specs, npb = _split_rows_specs(xp, xs, tm, d, 0, lambda i: 0)
    return pl.pallas_call(
        functools.partial(_rmsnorm_kernel, npb=npb),
        grid=(t // tm,),
        in_specs=specs + [pl.BlockSpec((1, d), lambda i: (0, 0))],
        out_specs=pl.BlockSpec((tm, d), lambda i: (i, 0)),
        out_shape=jax.ShapeDtypeStruct((t, d), out_dtype),
        compiler_params=_cparams(1),
        name="rmsnorm",
    )(xp, xs, g.reshape(1, d))


def _mm_kernel(*refs, n_a, has_res, res_npb, fold_out):
    a_refs = refs[:n_a]
    w_ref = refs[n_a]
    pos = n_a + 1
    if has_res:
        resp_ref, ress_ref = refs[pos:pos + 2]
        pos += 2
    o_ref = refs[pos]
    w_bf = refs[pos + 1]

    @pl.when(pl.program_id(1) == 0)
    def _():
        _cast_rows(w_ref, w_bf, 256)

    acc = None
    k0 = 0
    for a_ref in a_refs:
        ka = a_ref.shape[1]
        part = jnp.dot(a_ref[...], w_bf[k0:k0 + ka, :], preferred_element_type=F32)
        acc = part if acc is None else acc + part
        k0 += ka
    if has_res:
        acc = acc + jnp.where(pl.program_id(1) < res_npb, resp_ref[...], ress_ref[...])
    if not fold_out:
        o_ref[...] = acc.astype(o_ref.dtype)
    else:
        slab = refs[pos + 2]
        tm = acc.shape[0]
        for k in range(acc.shape[1] // LANES):
            slab[k] = acc[:, k * LANES:(k + 1) * LANES]
        for k in range(acc.shape[1] // LANES):
            for s in range(FOLD):
                o_ref[k, :, s * LANES:(s + 1) * LANES] = slab[k, pl.ds(s, tm // FOLD, stride=FOLD), :]


def _matmul(a_list, w, col0, ncols, *, res=None, fold_out=False, out_dtype=F32, tm_target=512, tn=512):
    t = a_list[0].shape[0]
    k_total = sum(a.shape[1] for a in a_list)
    assert w.shape[0] == k_total
    tn = math.gcd(math.gcd(ncols, col0), tn)
    assert tn % LANES == 0
    rows = t if res is None else math.gcd(res[0].shape[0], res[1].shape[0])
    tm = _pick_tile(rows, tm_target, SUBLANES * FOLD if fold_out else 16)
    nj, ni = ncols // tn, t // tm
    j0 = col0 // tn
    in_specs = [pl.BlockSpec((tm, a.shape[1]), lambda j, i: (i, 0)) for a in a_list]
    in_specs.append(pl.BlockSpec((k_total, tn), lambda j, i: (0, j + j0)))
    args = list(a_list) + [w]
    res_npb = 0
    if res is not None:
        specs, res_npb = _split_rows_specs(res[0], res[1], tm, tn, 1, lambda j, i: j)
        in_specs += specs
        args += list(res)
    scratch = [pltpu.VMEM((k_total, tn), BF16)]
    if fold_out:
        nk = tn // LANES
        out_shape = jax.ShapeDtypeStruct((ncols // LANES, t // FOLD, FOLD * LANES), F32)
        out_spec = pl.BlockSpec((nk, tm // FOLD, FOLD * LANES), lambda j, i: (j, i, 0))
        scratch.append(pltpu.VMEM((nk, tm, LANES), F32))
    else:
        out_shape = jax.ShapeDtypeStruct((t, ncols), out_dtype)
        out_spec = pl.BlockSpec((tm, tn), lambda j, i: (i, j))
    return pl.pallas_call(
        functools.partial(_mm_kernel, n_a=len(a_list), has_res=res is not None, res_npb=res_npb, fold_out=fold_out),
        grid=(nj, ni),
        in_specs=in_specs,
        out_specs=out_spec,
        out_shape=out_shape,
        scratch_shapes=scratch,
        compiler_params=_cparams(2),
        name="dense_matmul",
    )(*args)


def _tile_blockdiag(x, rep, col_group):
    r, c = x.shape[-2:]
    xt = jnp.broadcast_to(x[..., None, :, :], x.shape[:-2] + (rep, r, c))
    a_idx = lax.broadcasted_iota(jnp.int32, (rep, r, c), 0)
    c_idx = lax.broadcasted_iota(jnp.int32, (rep, r, c), 2) // col_group
    return jnp.where(a_idx == c_idx, xt, 0.0).reshape(x.shape[:-2] + (rep * r, c))


def _s5_tables(a_re, a_im, log_dt, b_re, b_im, c_re, c_im, d_skip, ls):
    hi = lax.Precision.HIGHEST
    g, p, gc = b_re.shape
    gt = LANES // gc
    nt = g // gt
    cs = FOLD
    dt = jnp.exp(log_dt.astype(F32))[:, None]
    ar, ai = a_re.astype(F32) * dt, a_im.astype(F32) * dt
    mag = jnp.exp(ar)
    lr, li = mag * jnp.cos(ai), mag * jnp.sin(ai)
    den = a_re * a_re + a_im * a_im
    half = jnp.sin(0.5 * ai)
    nr, ni = jnp.expm1(ar) * jnp.cos(ai) - 2.0 * half * half, li
    fr, fi = (nr * a_re + ni * a_im) / den, (ni * a_re - nr * a_im) / den
    bbr = fr[..., None] * b_re - fi[..., None] * b_im
    bbi = fr[..., None] * b_im + fi[..., None] * b_re
    taus = jnp.arange(cs + 1, dtype=F32)[:, None, None]
    pmag = jnp.exp(taus * ar)
    pr, pi = pmag * jnp.cos(taus * ai), pmag * jnp.sin(taus * ai)
    wr = pr[..., None] * bbr - pi[..., None] * bbi
    wi = pr[..., None] * bbi + pi[..., None] * bbr
    kt = (jnp.einsum('gcp,tgpd->tgcd', c_re, wr[:cs], precision=hi)
          - jnp.einsum('gcp,tgpd->tgcd', c_im, wi[:cs], precision=hi))
    def blocks(x, rows, cols):
        x = x.reshape(x.shape[0], nt, gt, cols, rows).transpose(0, 1, 4, 2, 3)
        return _tile_blockdiag(x.reshape(x.shape[0], nt, rows, gt * cols), gt, cols).astype(BF16)

    bd = blocks(kt, gc, gc)
    zero = jnp.zeros_like(bd[0])
    m_loc = jnp.concatenate(
        [jnp.concatenate([zero] * s + [bd[tau] for tau in range(cs - s)], axis=2) for s in range(cs)], axis=1)
    er = blocks(wr[:cs], gc, p)
    ei = blocks(wi[:cs], gc, p)
    m_end = jnp.concatenate(
        [jnp.concatenate([er[cs - 1 - s], ei[cs - 1 - s]], axis=2) for s in range(cs)], axis=1)
    cr = c_re[None] * pr[1:, :, None, :] - c_im[None] * pi[1:, :, None, :]
    ci = c_re[None] * pi[1:, :, None, :] + c_im[None] * pr[1:, :, None, :]
    cbr = blocks(cr, p, gc)
    cbi = blocks(-ci, p, gc)
    m_car = jnp.concatenate([jnp.concatenate([cbr[t] for t in range(cs)], axis=2),
                             jnp.concatenate([cbi[t] for t in range(cs)], axis=2)], axis=1)
    dvec = jnp.broadcast_to(d_skip.astype(F32).reshape(nt, 1, 1, gt * gc), (nt, 1, cs, gt * gc)).reshape(nt, 1, cs * gt * gc)
    lam = lambda n: (pr[n].reshape(nt, 1, gt * p), pi[n].reshape(nt, 1, gt * p))
    return (m_loc, m_end, m_car, dvec) + lam(cs) + lam(ls)


def _gelu_tanh(x):
    return 0.5 * x * (1.0 + jnp.tanh(math.sqrt(2.0 / math.pi) * (x + 0.044715 * (x * x * x))))


def _s5_seq_kernel(u_ref, ml_ref, me_ref, mc_ref, dv_ref, lr_ref, li_ref, z_ref, st_ref, el_scr, hin_scr):
    u = u_ref[0]
    ub = u.astype(BF16)
    rows = u.shape[0]
    hw = lr_ref.shape[2]
    y = jnp.dot(ub, ml_ref[0], preferred_element_type=F32)
    el_scr[...] = jnp.dot(ub, me_ref[0], preferred_element_type=F32)
    lam_r, lam_i = lr_ref[0], li_ref[0]

    def body(n, carry):
        hr, hi_ = carry
        hin_scr[pl.ds(n, 1), 0:hw] = hr
        hin_scr[pl.ds(n, 1), hw:2 * hw] = hi_
        e = el_scr[pl.ds(n, 1), :]
        return (lam_r * hr - lam_i * hi_ + e[:, 0:hw], lam_r * hi_ + lam_i * hr + e[:, hw:2 * hw])

    zero = jnp.zeros((1, hw), F32)
    hr, hi_ = lax.fori_loop(0, rows, body, (zero, zero))
    st_ref[0, 0, :, 0:hw] = hr
    st_ref[0, 0, :, hw:2 * hw] = hi_
    y = y + jnp.dot(hin_scr[...].astype(BF16), mc_ref[0], preferred_element_type=F32) + dv_ref[0] * u
    z_ref[0] = _gelu_tanh(y)


def _s5_batch_kernel(zprev_ref, u_ref, h0_ref, ml_ref, me_ref, mc_ref, dv_ref, lr_ref, li_ref, z_ref, st_ref, *, nseq):
    del zprev_ref
    hw = lr_ref.shape[2]
    wq = ml_ref.shape[1]
    lam_r, lam_i = lr_ref[0], li_ref[0]
    for q in range(nseq):
        u = u_ref[0, :, q * wq:(q + 1) * wq]
        ub = u.astype(BF16)
        o = 2 * hw * q
        h0 = h0_ref[0, :, o:o + 2 * hw]
        y = jnp.dot(ub, ml_ref[0], preferred_element_type=F32)
        y = y + jnp.dot(h0.astype(BF16), mc_ref[0], preferred_element_type=F32) + dv_ref[0] * u
        z_ref[0, :, q * wq:(q + 1) * wq] = _gelu_tanh(y)
        el = jnp.dot(ub, me_ref[0], preferred_element_type=F32)
        hr, hi_ = h0[:, 0:hw], h0[:, hw:2 * hw]
        st_ref[0, :, o:o + hw] = lam_r * hr - lam_i * hi_ + el[:, 0:hw]
        st_ref[0, :, o + hw:o + 2 * hw] = lam_r * hi_ + lam_i * hr + el[:, hw:2 * hw]


def _s5_mixer(u_fold, ssm, h0_re, h0_im, bp, lp, bs, ls):
    a_re, a_im, log_dt, b_re, b_im, c_re, c_im, d_skip = ssm
    g, p, gc = b_re.shape
    gt = LANES // gc
    nt, tf, wf = u_fold.shape
    assert g % gt == 0 and nt == g // gt and wf == FOLD * LANES
    assert lp % FOLD == 0 and FOLD % ls == 0
    nseq = FOLD // ls
    assert bs % nseq == 0
    rp = lp // FOLD
    rs = bs // nseq
    tp_f = bp * rp
    assert tp_f % rs == 0 and tf == tp_f + rs
    hw = gt * p

    m_loc, m_end, m_car, dvec, lr_c, li_c, lr_s, li_s = _s5_tables(
        a_re, a_im, log_dt, b_re, b_im, c_re, c_im, d_skip, ls)
    tabs = (m_loc, m_end, m_car, dvec, lr_c, li_c)
    tile = lambda shape: pl.BlockSpec((1,) + shape, lambda j, b: (j, 0, 0))
    z_p, st_p = pl.pallas_call(
        _s5_seq_kernel,
        grid=(nt, bp),
        in_specs=[pl.BlockSpec((1, rp, wf), lambda j, b: (j, b, 0)),
                  tile((wf, wf)), tile((wf, 2 * hw)), tile((2 * hw, wf)),
                  tile((1, wf)), tile((1, hw)), tile((1, hw))],
        out_specs=[pl.BlockSpec((1, rp, wf), lambda j, b: (j, b, 0)),
                   pl.BlockSpec((1, 1, 1, 2 * hw), lambda j, b: (j, b, 0, 0))],
        out_shape=[jax.ShapeDtypeStruct((nt, tf, wf), F32),
                   jax.ShapeDtypeStruct((nt, bp, 1, 2 * hw), F32)],
        scratch_shapes=[pltpu.VMEM((rp, 2 * hw), F32), pltpu.VMEM((rp, 2 * hw), F32)],
        compiler_params=_cparams(2),
        name="s5_prompt",
    )(u_fold, *tabs)

    h0 = jnp.stack([h0_re.astype(F32), h0_im.astype(F32)], axis=1)
    h0 = h0.reshape(rs, nseq, 2, nt, gt * p).transpose(3, 0, 1, 2, 4).reshape(nt, rs, nseq * 2 * hw)
    sblk = tp_f // rs
    tile1 = lambda shape: pl.BlockSpec((1,) + shape, lambda j: (j, 0, 0))
    wq = ls * LANES
    z_all, st_s = pl.pallas_call(
        functools.partial(_s5_batch_kernel, nseq=nseq),
        grid=(nt,),
        in_specs=[pl.BlockSpec(memory_space=pl.ANY),
                  pl.BlockSpec((1, rs, wf), lambda j: (j, sblk, 0)),
                  tile1((rs, nseq * 2 * hw)),
                  tile1((wq, wq)),
                  pl.BlockSpec((1, wq, 2 * hw), lambda j: (j, nseq - 1, 0)),
                  tile1((2 * hw, wq)),
                  tile1((1, wq)), tile1((1, hw)), tile1((1, hw))],
        out_specs=[pl.BlockSpec((1, rs, wf), lambda j: (j, sblk, 0)),
                   tile1((rs, nseq * 2 * hw))],
        out_shape=[jax.ShapeDtypeStruct((nt, tf, wf), F32),
                   jax.ShapeDtypeStruct((nt, rs, nseq * 2 * hw), F32)],
        input_output_aliases={0: 0},
        compiler_params=_cparams(1),
        name="s5_sample",
    )(z_p, u_fold, h0, m_loc, m_end, m_car, dvec, lr_s, li_s)

    st_p = st_p.reshape(nt, bp, 2, gt, p).transpose(2, 1, 0, 3, 4).reshape(2, bp, g, p)
    st_s = st_s.reshape(nt, rs, nseq, 2, gt, p).transpose(3, 1, 2, 0, 4, 5).reshape(2, bs, g, p)
    return z_all, st_p[0], st_p[1], st_s[0], st_s[1]


def _glu_kernel(zf_ref, w_ref, b_ref, o_ref, w_bf, z_nat):
    j = pl.program_id(0)

    @pl.when(pl.program_id(1) == 0)
    def _():
        _cast_rows(w_ref, w_bf, 256)

    nt, rf, _ = zf_ref.shape
    tm = rf * FOLD
    for k in range(nt):
        for t in range(FOLD):
            z_nat[k, pl.ds(t, rf, stride=FOLD), :] = zf_ref[k, :, t * LANES:(t + 1) * LANES]
    z = jnp.concatenate([z_nat[k] for k in range(nt)], axis=-1)
    tn = o_ref.shape[1]
    acc = jnp.dot(z.astype(BF16), w_bf[...], preferred_element_type=F32) + b_ref[...]
    nk = tn // LANES
    for kk in range(nk):
        zt = z_nat[j * nk + kk]
        o_ref[:, kk * LANES:(kk + 1) * LANES] = (zt * _sigmoid(acc[:, kk * LANES:(kk + 1) * LANES])).astype(o_ref.dtype)
    del tm


def _glu(z_fold, w_glu, b_glu):
    nt, tf, wf = z_fold.shape
    t = tf * FOLD
    n = nt * LANES
    tn = min(512, n)
    tm = _pick_tile(t, 512, SUBLANES * FOLD)
    return pl.pallas_call(
        _glu_kernel,
        grid=(n // tn, t // tm),
        in_specs=[pl.BlockSpec((nt, tm // FOLD, wf), lambda j, i: (0, i, 0)),
                  pl.BlockSpec((n, tn), lambda j, i: (0, j)),
                  pl.BlockSpec((1, tn), lambda j, i: (0, j))],
        out_specs=pl.BlockSpec((tm, tn), lambda j, i: (i, j)),
        out_shape=jax.ShapeDtypeStruct((t, n), BF16),
        scratch_shapes=[pltpu.VMEM((n, tn), BF16), pltpu.VMEM((nt, tm, LANES), F32)],
        compiler_params=_cparams(2),
        name="s5_glu",
    )(z_fold, w_glu, b_glu.reshape(1, n))


def _hg_gates(qin, fin, lb):
    q = qin * _sigmoid(qin)
    fg = lb + (1.0 - lb) * _sigmoid(fin)
    return q, fg, jnp.log(fg), 1.0 - fg


def _cumsum_rows(logf, tri_bf):
    h1, h2, h3 = _split3(logf)
    return (jnp.dot(tri_bf, h1, preferred_element_type=F32) + jnp.dot(tri_bf, h2, preferred_element_type=F32)
            + jnp.dot(tri_bf, h3, preferred_element_type=F32))


def _hg_out(o, gin, gain):
    o = o * lax.rsqrt(jnp.mean(o * o, axis=-1, keepdims=True) + NORM_EPS) * gain
    return o * (gin * _sigmoid(gin))


def _hg_prompt_kernel(q_ref, f_ref, i_ref, g_ref, lb_ref, gain_ref, o_ref, s_ref, st):
    n = pl.program_id(2)
    c, sub = HG_CHUNK, HG_SUB
    nsub = c // sub

    @pl.when(n == 0)
    def _():
        st[...] = jnp.zeros_like(st)

    lb = lb_ref[0]
    gain = gain_ref[...]
    ri = lax.broadcasted_iota(jnp.int32, (c, c), 0)
    ci = lax.broadcasted_iota(jnp.int32, (c, c), 1)
    tri = (ri >= ci).astype(BF16)
    t_io = lax.broadcasted_iota(jnp.int32, (sub, LANES), 0)
    nt_dims = (((1,), (1,)), ((), ()))
    tn_dims = (((0,), (0,)), ((), ()))

    for cc in range(q_ref.shape[0] // c):
        rows = slice(cc * c, (cc + 1) * c)
        q, fg, logf, k = _hg_gates(q_ref[rows, :], f_ref[rows, :], lb)
        v = i_ref[rows, :]
        b = _cumsum_rows(logf, tri)
        vb = v.astype(BF16)
        s_t = st[...]
        o_parts = []
        for i in range(nsub):
            sl = slice(i * sub, (i + 1) * sub)
            b_i, q_i, k_i, v_i = b[sl], q[sl], k[sl], v[sl]
            r_i = b[i * sub - 1:i * sub] if i > 0 else jnp.zeros((1, LANES), F32)
            o_i = jnp.zeros((sub, LANES), F32)
            for s in range(sub):
                arg = jnp.where(t_io >= s, b_i - b_i[s:s + 1], -jnp.inf)
                w = q_i * k_i[s:s + 1] * jnp.exp(arg)
                col = jnp.sum(w, axis=-1, keepdims=True)
                o_i = o_i + col * v_i[s:s + 1]
            if i > 0:
                qt = (q_i * jnp.exp(b_i - r_i)).astype(BF16)
                kfull = (k[0:i * sub] * jnp.exp(r_i - b[0:i * sub])).astype(BF16)
                a = lax.dot_general(qt, kfull, nt_dims, preferred_element_type=F32)
                o_i = o_i + jnp.dot(a.astype(BF16), vb[0:i * sub], preferred_element_type=F32)
            o_parts.append(o_i)
        o = jnp.concatenate(o_parts, axis=0)
        qe = (q * jnp.exp(b)).astype(BF16)
        o = o + lax.dot_general(qe, s_t.astype(BF16), nt_dims, preferred_element_type=F32)
        b_last = b[c - 1:c]
        khat = (k * jnp.exp(b_last - b)).astype(BF16)
        st[...] = s_t * jnp.exp(b_last) + lax.dot_general(vb, khat, tn_dims, preferred_element_type=F32)
        o_ref[rows, :] = _hg_out(o, g_ref[rows, :], gain).astype(o_ref.dtype)

    @pl.when(n == pl.num_programs(2) - 1)
    def _():
        s_ref[0, 0] = st[...].T


def _hg_sample_kernel(hprev_ref, q_ref, f_ref, i_ref, g_ref, lb_ref, gain_ref, s0_ref, o_ref, s_ref, *, ls):
    del hprev_ref
    nseq = s0_ref.shape[0]
    rows = nseq * ls
    lb = lb_ref[0]
    q, fg, logf, k = _hg_gates(q_ref[...], f_ref[...], lb)
    v = i_ref[...]
    ri = lax.broadcasted_iota(jnp.int32, (rows, rows), 0)
    ci = lax.broadcasted_iota(jnp.int32, (rows, rows), 1)
    tri = ((ri >= ci) & (ri // ls == ci // ls)).astype(BF16)
    b = _cumsum_rows(logf, tri)
    step = lax.broadcasted_iota(jnp.int32, (rows, LANES), 0) % ls
    o = jnp.zeros((rows, LANES), F32)
    for d in range(ls):
        if d == 0:
            ks, bs_, vs = k, b, v
        else:
            ks, bs_, vs = pltpu.roll(k, d, 0), pltpu.roll(b, d, 0), pltpu.roll(v, d, 0)
        arg = jnp.where(step >= d, b - bs_, -jnp.inf)
        col = jnp.sum(q * ks * jnp.exp(arg), axis=-1, keepdims=True)
        o = o + col * vs
    qe = (q * jnp.exp(b)).astype(BF16)
    tn_dims = (((0,), (0,)), ((), ()))
    last = ((ci == (ri // ls) * ls + ls - 1)).astype(BF16)
    b_end = _cumsum_rows(b, last)
    khat = k * jnp.exp(b_end - b)
    e_end = jnp.exp(b_end)
    seq_of_row = lax.broadcasted_iota(jnp.int32, (rows, LANES), 0) // ls
    erow = lax.broadcasted_iota(jnp.int32, (2 * SUBLANES, LANES), 0)
    ones = jnp.ones((2 * SUBLANES, LANES), BF16)
    vb = v.astype(BF16)
    for r in range(nseq):
        mine = seq_of_row == r
        s0 = s0_ref[r, 0]
        o = o + jnp.where(mine, jnp.dot(qe, s0.astype(BF16), preferred_element_type=F32), 0.0)
        kv = lax.dot_general(jnp.where(mine, khat, 0.0).astype(BF16), vb, tn_dims, preferred_element_type=F32)
        e_r = e_end[r * ls:r * ls + 1]
        e_hi = e_r.astype(BF16).astype(F32)
        emat = jnp.where(erow == 0, e_hi, jnp.where(erow == 1, e_r - e_hi, 0.0)).astype(BF16)
        decay = lax.dot_general(emat, ones, tn_dims, preferred_element_type=F32)
        s_ref[r, 0] = s0 * decay + kv
    o_ref[...] = _hg_out(o, g_ref[...], gain_ref[...]).astype(o_ref.dtype)


def _hgrn_mixer(qfig, lb, gain, s0, bp, lp, bs, ls):
    t = qfig.shape[0]
    h, dk = lb.shape
    assert dk == LANES and qfig.shape[1] == 4 * h * dk
    rb = _pick_tile(lp, 256, HG_CHUNK)
    nb = lp // rb
    lb3 = lb.reshape(h, 1, dk)
    gain2 = gain.reshape(1, dk)
    col = lambda off: pl.BlockSpec((rb, dk), lambda b, hh, n: (b * nb + n, off * h + hh))
    out_p, s_p = pl.pallas_call(
        _hg_prompt_kernel,
        grid=(bp, h, nb),
        in_specs=[col(0), col(1), col(2), col(3),
                  pl.BlockSpec((1, 1, dk), lambda b, hh, n: (hh, 0, 0)),
                  pl.BlockSpec((1, dk), lambda b, hh, n: (0, 0))],
        out_specs=[pl.BlockSpec((rb, dk), lambda b, hh, n: (b * nb + n, hh)),
                   pl.BlockSpec((1, 1, dk, dk), lambda b, hh, n: (b, hh, 0, 0))],
        out_shape=[jax.ShapeDtypeStruct((t, h * dk), BF16),
                   jax.ShapeDtypeStruct((bp, h, dk, dk), F32)],
        scratch_shapes=[pltpu.VMEM((dk, dk), F32)],
        compiler_params=_cparams(3),
        name="hgrn_prompt",
    )(qfig, qfig, qfig, qfig, lb3, gain2)

    sb = _pick_tile(bs, 8, 1)
    rows = sb * ls
    assert rows % 16 == 0 and (bp * lp) % rows == 0
    r0 = (bp * lp) // rows
    scol = lambda off: pl.BlockSpec((rows, dk), lambda i, hh: (r0 + i, off * h + hh))
    out_all, s_s = pl.pallas_call(
        functools.partial(_hg_sample_kernel, ls=ls),
        grid=(bs // sb, h),
        in_specs=[pl.BlockSpec(memory_space=pl.ANY),
                  scol(0), scol(1), scol(2), scol(3),
                  pl.BlockSpec((1, 1, dk), lambda i, hh: (hh, 0, 0)),
                  pl.BlockSpec((1, dk), lambda i, hh: (0, 0)),
                  pl.BlockSpec((sb, 1, dk, dk), lambda i, hh: (i, hh, 0, 0))],
        out_specs=[pl.BlockSpec((rows, dk), lambda i, hh: (r0 + i, hh)),
                   pl.BlockSpec((sb, 1, dk, dk), lambda i, hh: (i, hh, 0, 0))],
        out_shape=[jax.ShapeDtypeStruct((t, h * dk), BF16),
                   jax.ShapeDtypeStruct((bs, h, dk, dk), F32)],
        input_output_aliases={0: 0},
        compiler_params=_cparams(2),
        name="hgrn_sample",
    )(out_p, qfig, qfig, qfig, qfig, lb3, gain2, s0)
    return out_all, s_p, s_s


def _router_kernel(x_ref, g_ref, w_ref, b_ref, t_ref, idx_ref, gate_ref):
    x = x_ref[...]
    ms = jnp.mean(x * x, axis=-1, keepdims=True)
    tok = x * lax.rsqrt(ms + NORM_EPS) * g_ref[...]
    half = tok.shape[1] // 2
    lo = lax.bitcast_convert_type(tok[:, :half].astype(BF16).astype(F32), jnp.uint32)
    hi = lax.bitcast_convert_type(tok[:, half:].astype(BF16).astype(F32), jnp.uint32)
    t_ref[...] = (lo >> 16) | (hi & jnp.uint32(0xFFFF0000))
    t_hi = tok.astype(BF16)
    t_lo = (tok - t_hi.astype(F32)).astype(BF16)
    w = w_ref[...]
    w_hi = w.astype(BF16)
    w_lo = (w - w_hi.astype(F32)).astype(BF16)
    logits = (jnp.dot(t_hi, w_hi, preferred_element_type=F32) + jnp.dot(t_hi, w_lo, preferred_element_type=F32)
              + jnp.dot(t_lo, w_hi, preferred_element_type=F32)) + b_ref[...]
    ne = logits.shape[1]
    lane = lax.broadcasted_iota(jnp.int32, logits.shape, 1).astype(F32)
    out_lane = lax.broadcasted_iota(jnp.int32, idx_ref.shape, 1)
    idx_out = jnp.zeros(idx_ref.shape, jnp.int32)
    val_out = jnp.zeros(idx_ref.shape, F32)
    vals = []
    cur = logits
    for r in range(TOP_K):
        m = jnp.max(cur, axis=-1, keepdims=True)
        sel = jnp.min(jnp.where(cur == m, lane, float(ne)), axis=-1, keepdims=True)
        cur = jnp.where(lane == sel, -jnp.inf, cur)
        vals.append(m)
        idx_out = jnp.where(out_lane == r, sel.astype(jnp.int32), idx_out)
    denom = sum(jnp.exp(v - vals[0]) for v in vals)
    for r in range(TOP_K):
        val_out = jnp.where(out_lane == r, jnp.exp(vals[r] - vals[0]) / denom, val_out)
    idx_ref[...] = idx_out
    gate_ref[...] = val_out


def _router(x1, g_ffn, w_router, b_router):
    t, d = x1.shape
    ne = w_router.shape[1]
    tm = _pick_tile(t, 256, 16)
    return pl.pallas_call(
        _router_kernel,
        grid=(t // tm,),
        in_specs=[pl.BlockSpec((tm, d), lambda i: (i, 0)), pl.BlockSpec((1, d), lambda i: (0, 0)),
                  pl.BlockSpec((d, ne), lambda i: (0, 0)), pl.BlockSpec((1, ne), lambda i: (0, 0))],
        out_specs=[pl.BlockSpec((tm, d // 2), lambda i: (i, 0)), pl.BlockSpec((tm, LANES), lambda i: (i, 0)),
                   pl.BlockSpec((tm, LANES), lambda i: (i, 0))],
        out_shape=[jax.ShapeDtypeStruct((t, d // 2), jnp.uint32), jax.ShapeDtypeStruct((t, LANES), jnp.int32),
                   jax.ShapeDtypeStruct((t, LANES), F32)],
        compiler_params=_cparams(1),
        name="router",
    )(x1, g_ffn.reshape(1, d), w_router, b_router.reshape(1, ne))


def _row_copy(src_hbm, dst_vmem, src_row, dst_row, sem):
    return pltpu.make_async_copy(src_hbm.at[pl.ds(src_row, 1), :], dst_vmem.at[pl.ds(dst_row, 1), :], sem)


def _gather_kernel(tok_tab, src_hbm, o_ref, buf, sem):
    bm = buf.shape[1]
    i = pl.program_id(0)

    def issue(step, slot):
        def start(r, c):
            _row_copy(src_hbm, buf.at[slot], tok_tab[step * bm + r], r, sem.at[slot]).start()
            return c
        lax.fori_loop(0, bm, start, 0, unroll=8)

    @pl.when(i == 0)
    def _():
        issue(0, 0)

    @pl.when(i + 1 < pl.num_programs(0))
    def _():
        issue(i + 1, (i + 1) % 2)

    slot = i % 2

    def wait(r, c):
        _row_copy(src_hbm, buf.at[slot], 0, r, sem.at[slot]).wait()
        return c

    lax.fori_loop(0, bm, wait, 0, unroll=8)
    w = buf[slot]
    half = w.shape[1]
    o_ref[:, :half] = lax.bitcast_convert_type(w << 16, F32).astype(o_ref.dtype)
    o_ref[:, half:] = lax.bitcast_convert_type(w & jnp.uint32(0xFFFF0000), F32).astype(o_ref.dtype)


def _gather_rows(src, row_tok, bm):
    p = row_tok.shape[0]
    d = 2 * src.shape[1]
    return pl.pallas_call(
        _gather_kernel,
        grid_spec=pltpu.PrefetchScalarGridSpec(
            num_scalar_prefetch=1,
            grid=(p // bm,),
            in_specs=[pl.BlockSpec(memory_space=pl.ANY)],
            out_specs=pl.BlockSpec((bm, d), lambda i, tab: (i, 0)),
            scratch_shapes=[pltpu.VMEM((2, bm, d // 2), src.dtype), pltpu.SemaphoreType.DMA((2,))],
        ),
        out_shape=jax.ShapeDtypeStruct((p, d), BF16),
        compiler_params=_cparams(1),
        name="moe_gather",
    )(row_tok, src)


def _moe_group_kernel(e_tab, blk_tab, nb_tab, x_hbm, *refs, n_w):
    del e_tab
    w_refs = refs[:n_w]
    b_refs = refs[n_w:2 * n_w]
    o_hbm = refs[2 * n_w]
    xbuf = refs[2 * n_w + 1]
    w_bfs = refs[2 * n_w + 2:3 * n_w + 2]
    obuf, sem_x, sem_o = refs[3 * n_w + 2:]
    g, j = pl.program_id(0), pl.program_id(1)
    nb, blk0 = nb_tab[g], blk_tab[g]
    bm, tn = obuf.shape[1], obuf.shape[2]

    def rows_copy(b):
        return pltpu.make_async_copy(x_hbm.at[pl.ds((blk0 + b) * bm, bm), :],
                                     xbuf.at[pl.ds(b * bm, bm), :], sem_x)

    @pl.when((j == 0) & (nb > 0))
    def _():
        def start(b, c):
            rows_copy(b).start()
            return c

        def wait(b, c):
            rows_copy(b).wait()
            return c

        lax.fori_loop(0, nb, start, 0)
        lax.fori_loop(0, nb, wait, 0)

    def out_copy(b, slot):
        return pltpu.make_async_copy(obuf.at[slot], o_hbm.at[pl.ds((blk0 + b) * bm, bm), pl.ds(j * tn, tn)],
                                     sem_o.at[slot])

    @pl.when(nb > 0)
    def _():
        for w_ref, w_bf in zip(w_refs, w_bfs):
            _cast_rows(w_ref, w_bf, 256)

        def body(b, c):
            slot = b % 2

            @pl.when(b >= 2)
            def _():
                out_copy(b - 2, slot).wait()

            x = xbuf[pl.ds(pl.multiple_of(b * bm, bm), bm), :]
            acc = jnp.dot(x, w_bfs[0][...], preferred_element_type=F32) + b_refs[0][...]
            if n_w == 2:
                up = jnp.dot(x, w_bfs[1][...], preferred_element_type=F32) + b_refs[1][...]
                gate = jnp.minimum(acc, SWIGLU_LIMIT)
                up = jnp.clip(up, -SWIGLU_LIMIT, SWIGLU_LIMIT)
                acc = (up + 1.0) * gate * _sigmoid(SWIGLU_ALPHA * gate)
            obuf[slot] = acc.astype(obuf.dtype)
            out_copy(b, slot).start()
            return c

        lax.fori_loop(0, nb, body, 0)

        @pl.when(nb >= 2)
        def _():
            out_copy(nb - 2, nb % 2).wait()

        out_copy(nb - 1, (nb - 1) % 2).wait()


def _moe_group_tables(nb_e, n_groups):
    ne = nb_e.shape[0]
    gb = MOE_GROUP_BLOCKS
    ng_e = (nb_e + gb - 1) // gb
    g_end = jnp.cumsum(ng_e)
    g_start = g_end - ng_e
    total = g_end[-1]
    blk_start = jnp.cumsum(nb_e) - nb_e
    gi = jnp.arange(n_groups, dtype=jnp.int32)
    gc = jnp.minimum(gi, total - 1)
    e = jnp.minimum(jnp.sum((gc[:, None] >= g_end[None, :]).astype(jnp.int32), axis=1), ne - 1)
    local = gc - g_start[e]
    blk0 = blk_start[e] + gb * local
    nb = jnp.where(gi < total, jnp.minimum(gb, nb_e[e] - gb * local), 0)
    return e.astype(jnp.int32), blk0.astype(jnp.int32), nb.astype(jnp.int32)


def _moe_grouped(x, tabs, ws, bs, out_dtype, bm, tn):
    p, k = x.shape
    ne, _, n = ws[0].shape
    n_w = len(ws)
    nj = n // tn
    n_groups = tabs[0].shape[0]
    frozen = lambda g, j, e, blk, nb: (e[g], 0, jnp.where(nb[g] > 0, j, nj - 1))
    return pl.pallas_call(
        functools.partial(_moe_group_kernel, n_w=n_w),
        grid_spec=pltpu.PrefetchScalarGridSpec(
            num_scalar_prefetch=3,
            grid=(n_groups, nj),
            in_specs=[pl.BlockSpec(memory_space=pl.ANY)]
            + [pl.BlockSpec((None, k, tn), frozen)] * n_w
            + [pl.BlockSpec((None, 1, tn), frozen)] * n_w,
            out_specs=pl.BlockSpec(memory_space=pl.ANY),
            scratch_shapes=[pltpu.VMEM((MOE_GROUP_BLOCKS * bm, k), x.dtype)]
            + [pltpu.VMEM((k, tn), BF16)] * n_w
            + [pltpu.VMEM((2, bm, tn), out_dtype), pltpu.SemaphoreType.DMA(()), pltpu.SemaphoreType.DMA((2,))],
        ),
        out_shape=jax.ShapeDtypeStruct((p, n), out_dtype),
        compiler_params=_cparams(2),
        name="moe_up" if n_w == 2 else "moe_down",
    )(*tabs, x, *ws, *[b.reshape(ne, 1, n) for b in bs])


def _moe_experts(xs, nb_e, w_gate, b_gate, w_up, b_up, w_down, b_down, bm):
    p, d = xs.shape
    ne, _, f = w_gate.shape
    n_groups = ne + -(-(p // bm) // MOE_GROUP_BLOCKS)
    tabs = _moe_group_tables(nb_e, n_groups)
    hid = _moe_grouped(xs, tabs, (w_gate, w_up), (b_gate, b_up), BF16, bm, min(256, f))
    return _moe_grouped(hid, tabs, (w_down,), (b_down,), F32, bm, min(512, d))


def _combine_kernel(pos_tab, y_hbm, x_ref, gate_ref, g_ref, op_ref, os_ref, buf, sem, *, npb):
    tc = x_ref.shape[0]
    i = pl.program_id(0)

    def issue(step, slot):
        def start(r, c):
            for k in range(TOP_K):
                _row_copy(y_hbm, buf.at[slot, k], pos_tab[(step * tc + r) * TOP_K + k], r, sem.at[slot]).start()
            return c
        lax.fori_loop(0, tc, start, 0, unroll=2)

    @pl.when(i == 0)
    def _():
        issue(0, 0)

    @pl.when(i + 1 < pl.num_programs(0))
    def _():
        issue(i + 1, (i + 1) % 2)

    slot = i % 2

    def wait(r, c):
        for k in range(TOP_K):
            _row_copy(y_hbm, buf.at[slot, k], 0, r, sem.at[slot]).wait()
        return c

    lax.fori_loop(0, tc, wait, 0, unroll=2)
    gates = gate_ref[...]
    acc = x_ref[...]
    for k in range(TOP_K):
        acc = acc + gates[:, k:k + 1] * buf[slot, k]
    ms = jnp.mean(acc * acc, axis=-1, keepdims=True)
    out = acc * lax.rsqrt(ms + NORM_EPS) * g_ref[...]

    @pl.when(i < npb)
    def _():
        op_ref[...] = out

    @pl.when(i >= npb)
    def _():
        os_ref[...] = out


def _combine(y, pos, x1, gates, g_final, tp):
    t, d = x1.shape
    ts = t - tp
    tc = _pick_tile(math.gcd(tp, ts), 64, 8)
    npb = tp // tc
    return pl.pallas_call(
        functools.partial(_combine_kernel, npb=npb),
        grid_spec=pltpu.PrefetchScalarGridSpec(
            num_scalar_prefetch=1,
            grid=(t // tc,),
            in_specs=[pl.BlockSpec(memory_space=pl.ANY),
                      pl.BlockSpec((tc, d), lambda i, tab: (i, 0)),
                      pl.BlockSpec((tc, LANES), lambda i, tab: (i, 0)),
                      pl.BlockSpec((1, d), lambda i, tab: (0, 0))],
            out_specs=[pl.BlockSpec((tc, d), lambda i, tab: (jnp.minimum(i, npb - 1), 0)),
                       pl.BlockSpec((tc, d), lambda i, tab: (jnp.maximum(i - npb, 0), 0))],
            scratch_shapes=[pltpu.VMEM((2, TOP_K, tc, d), F32), pltpu.SemaphoreType.DMA((2,))],
        ),
        out_shape=[jax.ShapeDtypeStruct((tp, d), F32), jax.ShapeDtypeStruct((ts, d), F32)],
        compiler_params=_cparams(1),
        name="moe_combine",
    )(pos, y, x1, gates, g_final.reshape(1, d))


def kernel(x_prompt, x_sample, state_ssm_re, state_ssm_im, state_hgrn, g_mix, w_in, ssm_a_re, ssm_a_im, ssm_log_dt, ssm_b_re, ssm_b_im, ssm_c_re, ssm_c_im, ssm_d, w_glu, b_glu, hg_lb_logits, hg_o_gain, w_out, g_ffn, w_router, b_router, w_gate, b_gate, w_up, b_up, w_down, b_down, g_final):
    bp, lp, d = x_prompt.shape
    bs, ls, _ = x_sample.shape
    depth = w_in.shape[0]
    assert depth == 1
    tp, ts = bp * lp, bs * ls
    t = tp + ts
    s5w = ssm_b_re.shape[1] * ssm_b_re.shape[3]
    h, dk = hg_lb_logits.shape[1:]
    ne = w_router.shape[2]

    x0 = (x_prompt.reshape(tp, d), x_sample.reshape(ts, d))
    lower_bounds = jnp.cumsum(jax.nn.softmax(hg_lb_logits.astype(F32), axis=0), axis=0)

    hn = _rmsnorm(x0[0], x0[1], g_mix[0], BF16)
    u_fold = _matmul([hn], w_in[0], 0, s5w, fold_out=True)
    qfig = _matmul([hn], w_in[0], s5w, w_in.shape[2] - s5w)
    ssm = (ssm_a_re[0], ssm_a_im[0], ssm_log_dt[0], ssm_b_re[0], ssm_b_im[0], ssm_c_re[0], ssm_c_im[0], ssm_d[0])
    z_fold, re_p, im_p, re_s, im_s = _s5_mixer(u_fold, ssm, state_ssm_re[0], state_ssm_im[0], bp, lp, bs, ls)
    s5_out = _glu(z_fold, w_glu[0], b_glu[0])
    hg_out, s_p, s_s = _hgrn_mixer(qfig, lower_bounds[0], hg_o_gain[0], state_hgrn[0], bp, lp, bs, ls)
    x1 = _matmul([s5_out, hg_out], w_out[0], 0, d, res=x0)

    tok, idx, gates = _router(x1, g_ffn[0], w_router[0], b_router[0])
    m = t * TOP_K
    bm = MOE_BLOCK_ROWS
    flat_e = idx[:, :TOP_K].reshape(m)
    onehot = (flat_e[:, None] == jnp.arange(ne, dtype=jnp.int32)[None, :]).astype(jnp.int32)
    seen = jnp.cumsum(onehot, axis=0)
    counts = seen[-1]
    nb_e = (counts + bm - 1) // bm
    pad_end = jnp.cumsum(nb_e * bm)
    pad_start = pad_end - nb_e * bm
    pos = jnp.sum(onehot * (seen - 1 + pad_start[None, :]), axis=1).astype(jnp.int32)
    n_blocks = -(-(m + ne * (bm - 1)) // bm)
    p_rows = n_blocks * bm
    row_tok = jnp.zeros((p_rows,), jnp.int32).at[pos].set(jnp.arange(m, dtype=jnp.int32) // TOP_K)
    xs = _gather_rows(tok, row_tok, bm)
    y_rows = _moe_experts(xs, nb_e, w_gate[0], b_gate[0], w_up[0], b_up[0], w_down[0], b_down[0], bm)
    y_p, y_s = _combine(y_rows, pos, x1, gates, g_final, tp)

    y_prompt = y_p.reshape(bp, lp, d)
    y_sample = y_s.reshape(bs, ls, d)
    sd = state_ssm_re.dtype
    return (y_prompt, y_sample, re_p[None].astype(sd), im_p[None].astype(sd), s_p[None].astype(state_hgrn.dtype),
            re_s[None].astype(sd), im_s[None].astype(sd), s_s[None].astype(state_hgrn.dtype))
```

```python
import functools
import math

import jax
import jax.numpy as jnp
from jax import lax
from jax.experimental import pallas as pl
from jax.experimental.pallas import tpu as pltpu

F32 = jnp.float32
BF16 = jnp.bfloat16
TOP_K = 4
NORM_EPS = 1e-5
SWIGLU_LIMIT = 7.0
SWIGLU_ALPHA = 1.702
LANES = 128
SUBLANES = 8
FOLD = SUBLANES
HG_CHUNK = 64
HG_SUB = 16
MOE_BLOCK_ROWS = 256
MOE_GROUP_BLOCKS = 8
VMEM_LIMIT_BYTES = 56 * 1024 * 1024


def _cparams(n_axes):
    return pltpu.CompilerParams(dimension_semantics=("arbitrary",) * n_axes,
                                vmem_limit_bytes=VMEM_LIMIT_BYTES)


def _pick_tile(n, target, mult):
    best = None
    for t in range(mult, min(n, target) + 1, mult):
        if n % t == 0:
            best = t
    assert best is not None, (n, target, mult)
    return best


def _sigmoid(x):
    return 1.0 / (1.0 + jnp.exp(-x))


def _split3(x):
    h1 = x.astype(BF16)
    r1 = x - h1.astype(F32)
    h2 = r1.astype(BF16)
    r2 = r1 - h2.astype(F32)
    return h1, h2, r2.astype(BF16)


def _cast_rows(src_ref, dst_ref, chunk):
    rows = src_ref.shape[0]
    chunk = min(chunk, rows)
    assert rows % chunk == 0

    def body(r, c):
        sl = pl.ds(pl.multiple_of(r * chunk, chunk), chunk)
        dst_ref[sl, :] = src_ref[sl, :].astype(dst_ref.dtype)
        return c

    lax.fori_loop(0, rows // chunk, body, 0)


def _split_rows_specs(xp, xs, tm, width, row_axis, col_index):
    npb = xp.shape[0] // tm
    assert xp.shape[0] % tm == 0 and xs.shape[0] % tm == 0
    spec_p = pl.BlockSpec((tm, width), lambda *ids: (jnp.minimum(ids[row_axis], npb - 1), col_index(*ids)))
    spec_s = pl.BlockSpec((tm, width), lambda *ids: (jnp.maximum(ids[row_axis] - npb, 0), col_index(*ids)))
    return [spec_p, spec_s], npb


def _rmsnorm_kernel(xp_ref, xs_ref, g_ref, o_ref, *, npb):
    x = jnp.where(pl.program_id(0) < npb, xp_ref[...], xs_ref[...])
    ms = jnp.mean(x * x, axis=-1, keepdims=True)
    o_ref[...] = (x * lax.rsqrt(ms + NORM_EPS) * g_ref[...]).astype(o_ref.dtype)


def _rmsnorm(xp, xs, g, out_dtype):
    d = xp.shape[1]
    t = xp.shape[0] + xs.shape[0]
    tm = _pick_tile(math.gcd(xp.shape[0], xs.shape[0]), 256, 16)
    specs, npb = _split_rows_specs(xp, xs, tm, d, 0, lambda i: 0)
    return pl.pallas_call(
        functools.partial(_rmsnorm_kernel, npb=npb),
        grid=(t // tm,),
        in_specs=specs + [pl.BlockSpec((1, d), lambda i: (0, 0))],
        out_specs=pl.BlockSpec((tm, d), lambda i: (i, 0)),
        out_shape=jax.ShapeDtypeStruct((t, d), out_dtype),
        compiler_params=_cparams(1),
        name="rmsnorm",
    )(xp, xs, g.reshape(1, d))


def _mm_kernel(*refs, n_a, has_res, res_npb, fold_out):
    a_refs = refs[:n_a]
    w_ref = refs[n_a]
    pos = n_a + 1
    if has_res:
        resp_ref, ress_ref = refs[pos:pos + 2]
        pos += 2
    o_ref = refs[pos]
    w_bf = refs[pos + 1]

    @pl.when(pl.program_id(1) == 0)
    def _():
        _cast_rows(w_ref, w_bf, 256)

    acc = None
    k0 = 0
    for a_ref in a_refs:
        ka = a_ref.shape[1]
        part = jnp.dot(a_ref[...], w_bf[k0:k0 + ka, :], preferred_element_type=F32)
        acc = part if acc is None else acc + part
        k0 += ka
    if has_res:
        acc = acc + jnp.where(pl.program_id(1) < res_npb, resp_ref[...], ress_ref[...])
    if not fold_out:
        o_ref[...] = acc.astype(o_ref.dtype)
    else:
        slab = refs[pos + 2]
        tm = acc.shape[0]
        for k in range(acc.shape[1] // LANES):
            slab[k] = acc[:, k * LANES:(k + 1) * LANES]
        for k in range(acc.shape[1] // LANES):
            for s in range(FOLD):
                o_ref[k, :, s * LANES:(s + 1) * LANES] = slab[k, pl.ds(s, tm // FOLD, stride=FOLD), :]


def _matmul(a_list, w, col0, ncols, *, res=None, fold_out=False, out_dtype=F32, tm_target=512, tn=512):
    t = a_list[0].shape[0]
    k_total = sum(a.shape[1] for a in a_list)
    assert w.shape[0] == k_total
    tn = math.gcd(math.gcd(ncols, col0), tn)
    assert tn % LANES == 0
    rows = t if res is None else math.gcd(res[0].shape[0], res[1].shape[0])
    tm = _pick_tile(rows, tm_target, SUBLANES * FOLD if fold_out else 16)
    nj, ni = ncols // tn, t // tm
    j0 = col0 // tn
    in_specs = [pl.BlockSpec((tm, a.shape[1]), lambda j, i: (i, 0)) for a in a_list]
    in_specs.append(pl.BlockSpec((k_total, tn), lambda j, i: (0, j + j0)))
    args = list(a_list) + [w]
    res_npb = 0
    if res is not None:
        specs, res_npb = _split_rows_specs(res[0], res[1], tm, tn, 1, lambda j, i: j)
        in_specs += specs
        args += list(res)
    scratch = [pltpu.VMEM((k_total, tn), BF16)]
    if fold_out:
        nk = tn // LANES
        out_shape = jax.ShapeDtypeStruct((ncols // LANES, t // FOLD, FOLD * LANES), F32)
        out_spec = pl.BlockSpec((nk, tm // FOLD, FOLD * LANES), lambda j, i: (j, i, 0))
        scratch.append(pltpu.VMEM((nk, tm, LANES), F32))
    else:
        out_shape = jax.ShapeDtypeStruct((t, ncols), out_dtype)
        out_spec = pl.BlockSpec((tm, tn), lambda j, i: (i, j))
    return pl.pallas_call(
        functools.partial(_mm_kernel, n_a=len(a_list), has_res=res is not None, res_npb=res_npb, fold_out=fold_out),
        grid=(nj, ni),
        in_specs=in_specs,
        out_specs=out_spec,
        out_shape=out_shape,
        scratch_shapes=scratch,
        compiler_params=_cparams(2),
        name="dense_matmul",
    )(*args)


def _tile_blockdiag(x, rep, col_group):
    r, c = x.shape[-2:]
    xt = jnp.broadcast_to(x[..., None, :, :], x.shape[:-2] + (rep, r, c))
    a_idx = lax.broadcasted_iota(jnp.int32, (rep, r, c), 0)
    c_idx = lax.broadcasted_iota(jnp.int32, (rep, r, c), 2) // col_group
    return jnp.where(a_idx == c_idx, xt, 0.0).reshape(x.shape[:-2] + (rep * r, c))


def _s5_tables(a_re, a_im, log_dt, b_re, b_im, c_re, c_im, d_skip, ls):
    hi = lax.Precision.HIGHEST
    g, p, gc = b_re.shape
    gt = LANES // gc
    nt = g // gt
    cs = FOLD
    dt = jnp.exp(log_dt.astype(F32))[:, None]
    ar, ai = a_re.astype(F32) * dt, a_im.astype(F32) * dt
    mag = jnp.exp(ar)
    lr, li = mag * jnp.cos(ai), mag * jnp.sin(ai)
    den = a_re * a_re + a_im * a_im
    half = jnp.sin(0.5 * ai)
    nr, ni = jnp.expm1(ar) * jnp.cos(ai) - 2.0 * half * half, li
    fr, fi = (nr * a_re + ni * a_im) / den, (ni * a_re - nr * a_im) / den
    bbr = fr[..., None] * b_re - fi[..., None] * b_im
    bbi = fr[..., None] * b_im + fi[..., None] * b_re
    taus = jnp.arange(cs + 1, dtype=F32)[:, None, None]
    pmag = jnp.exp(taus * ar)
    pr, pi = pmag * jnp.cos(taus * ai), pmag * jnp.sin(taus * ai)
    wr = pr[..., None] * bbr - pi[..., None] * bbi
    wi = pr[..., None] * bbi + pi[..., None] * bbr
    kt = (jnp.einsum('gcp,tgpd->tgcd', c_re, wr[:cs], precision=hi)
          - jnp.einsum('gcp,tgpd->tgcd', c_im, wi[:cs], precision=hi))
    def blocks(x, rows, cols):
        x = x.reshape(x.shape[0], nt, gt, cols, rows).transpose(0, 1, 4, 2, 3)
        return _tile_blockdiag(x.reshape(x.shape[0], nt, rows, gt * cols), gt, cols).astype(BF16)

    bd = blocks(kt, gc, gc)
    zero = jnp.zeros_like(bd[0])
    m_loc = jnp.concatenate(
        [jnp.concatenate([zero] * s + [bd[tau] for tau in range(cs - s)], axis=2) for s in range(cs)], axis=1)
    er = blocks(wr[:cs], gc, p)
    ei = blocks(wi[:cs], gc, p)
    m_end = jnp.concatenate(
        [jnp.concatenate([er[cs - 1 - s], ei[cs - 1 - s]], axis=2) for s in range(cs)], axis=1)
    cr = c_re[None] * pr[1:, :, None, :] - c_im[None] * pi[1:, :, None, :]
    ci = c_re[None] * pi[1:, :, None, :] + c_im[None] * pr[1:, :, None, :]
    cbr = blocks(cr, p, gc)
    cbi = blocks(-ci, p, gc)
    m_car = jnp.concatenate([jnp.concatenate([cbr[t] for t in range(cs)], axis=2),
                             jnp.concatenate([cbi[t] for t in range(cs)], axis=2)], axis=1)
    dvec = jnp.broadcast_to(d_skip.astype(F32).reshape(nt, 1, 1, gt * gc), (nt, 1, cs, gt * gc)).reshape(nt, 1, cs * gt * gc)
    lam = lambda n: (pr[n].reshape(nt, 1, gt * p), pi[n].reshape(nt, 1, gt * p))
    return (m_loc, m_end, m_car, dvec) + lam(cs) + lam(ls)


def _gelu_tanh(x):
    return 0.5 * x * (1.0 + jnp.tanh(math.sqrt(2.0 / math.pi) * (x + 0.044715 * (x * x * x))))


def _s5_seq_kernel(u_ref, ml_ref, me_ref, mc_ref, dv_ref, lr_ref, li_ref, z_ref, st_ref, el_scr, hin_scr):
    u = u_ref[0]
    ub = u.astype(BF16)
    rows = u.shape[0]
    hw = lr_ref.shape[2]
    y = jnp.dot(ub, ml_ref[0], preferred_element_type=F32)
    el_scr[...] = jnp.dot(ub, me_ref[0], preferred_element_type=F32)
    lam_r, lam_i = lr_ref[0], li_ref[0]

    def body(n, carry):
        hr, hi_ = carry
        hin_scr[pl.ds(n, 1), 0:hw] = hr
        hin_scr[pl.ds(n, 1), hw:2 * hw] = hi_
        e = el_scr[pl.ds(n, 1), :]
        return (lam_r * hr - lam_i * hi_ + e[:, 0:hw], lam_r * hi_ + lam_i * hr + e[:, hw:2 * hw])

    zero = jnp.zeros((1, hw), F32)
    hr, hi_ = lax.fori_loop(0, rows, body, (zero, zero))
    st_ref[0, 0, :, 0:hw] = hr
    st_ref[0, 0, :, hw:2 * hw] = hi_
    y = y + jnp.dot(hin_scr[...].astype(BF16), mc_ref[0], preferred_element_type=F32) + dv_ref[0] * u
    z_ref[0] = _gelu_tanh(y)


def _s5_batch_kernel(zprev_ref, u_ref, h0_ref, ml_ref, me_ref, mc_ref, dv_ref, lr_ref, li_ref, z_ref, st_ref, *, nseq):
    del zprev_ref
    hw = lr_ref.shape[2]
    wq = ml_ref.shape[1]
    lam_r, lam_i = lr_ref[0], li_ref[0]
    for q in range(nseq):
        u = u_ref[0, :, q * wq:(q + 1) * wq]
        ub = u.astype(BF16)
        o = 2 * hw * q
        h0 = h0_ref[0, :, o:o + 2 * hw]
        y = jnp.dot(ub, ml_ref[0], preferred_element_type=F32)
        y = y + jnp.dot(h0.astype(BF16), mc_ref[0], preferred_element_type=F32) + dv_ref[0] * u
        z_ref[0, :, q * wq:(q + 1) * wq] = _gelu_tanh(y)
        el = jnp.dot(ub, me_ref[0], preferred_element_type=F32)
        hr, hi_ = h0[:, 0:hw], h0[:, hw:2 * hw]
        st_ref[0, :, o:o + hw] = lam_r * hr - lam_i * hi_ + el[:, 0:hw]
        st_ref[0, :, o + hw:o + 2 * hw] = lam_r * hi_ + lam_i * hr + el[:, hw:2 * hw]


def _s5_mixer(u_fold, ssm, h0_re, h0_im, bp, lp, bs, ls):
    a_re, a_im, log_dt, b_re, b_im, c_re, c_im, d_skip = ssm
    g, p, gc = b_re.shape
    gt = LANES // gc
    nt, tf, wf = u_fold.shape
    assert g % gt == 0 and nt == g // gt and wf == FOLD * LANES
    assert lp % FOLD == 0 and FOLD % ls == 0
    nseq = FOLD // ls
    assert bs % nseq == 0
    rp = lp // FOLD
    rs = bs // nseq
    tp_f = bp * rp
    assert tp_f % rs == 0 and tf == tp_f + rs
    hw = gt * p

    m_loc, m_end, m_car, dvec, lr_c, li_c, lr_s, li_s = _s5_tables(
        a_re, a_im, log_dt, b_re, b_im, c_re, c_im, d_skip, ls)
    tabs = (m_loc, m_end, m_car, dvec, lr_c, li_c)
    tile = lambda shape: pl.BlockSpec((1,) + shape, lambda j, b: (j, 0, 0))
    z_p, st_p = pl.pallas_call(
        _s5_seq_kernel,
        grid=(nt, bp),
        in_specs=[pl.BlockSpec((1, rp, wf), lambda j, b: (j, b, 0)),
                  tile((wf, wf)), tile((wf, 2 * hw)), tile((2 * hw, wf)),
                  tile((1, wf)), tile((1, hw)), tile((1, hw))],
        out_specs=[pl.BlockSpec((1, rp, wf), lambda j, b: (j, b, 0)),
                   pl.BlockSpec((1, 1, 1, 2 * hw), lambda j, b: (j, b, 0, 0))],
        out_shape=[jax.ShapeDtypeStruct((nt, tf, wf), F32),
                   jax.ShapeDtypeStruct((nt, bp, 1, 2 * hw), F32)],
        scratch_shapes=[pltpu.VMEM((rp, 2 * hw), F32), pltpu.VMEM((rp, 2 * hw), F32)],
        compiler_params=_cparams(2),
        name="s5_prompt",
    )(u_fold, *tabs)

    h0 = jnp.stack([h0_re.astype(F32), h0_im.astype(F32)], axis=1)
    h0 = h0.reshape(rs, nseq, 2, nt, gt * p).transpose(3, 0, 1, 2, 4).reshape(nt, rs, nseq * 2 * hw)
    sblk = tp_f // rs
    tile1 = lambda shape: pl.BlockSpec((1,) + shape, lambda j: (j, 0, 0))
    wq = ls * LANES
    z_all, st_s = pl.pallas_call(
        functools.partial(_s5_batch_kernel, nseq=nseq),
        grid=(nt,),
        in_specs=[pl.BlockSpec(memory_space=pl.ANY),
                  pl.BlockSpec((1, rs, wf), lambda j: (j, sblk, 0)),
                  tile1((rs, nseq * 2 * hw)),
                  tile1((wq, wq)),
                  pl.BlockSpec((1, wq, 2 * hw), lambda j: (j, nseq - 1, 0)),
                  tile1((2 * hw, wq)),
                  tile1((1, wq)), tile1((1, hw)), tile1((1, hw))],
        out_specs=[pl.BlockSpec((1, rs, wf), lambda j: (j, sblk, 0)),
                   tile1((rs, nseq * 2 * hw))],
        out_shape=[jax.ShapeDtypeStruct((nt, tf, wf), F32),
                   jax.ShapeDtypeStruct((nt, rs, nseq * 2 * hw), F32)],
        input_output_aliases={0: 0},
        compiler_params=_cparams(1),
        name="s5_sample",
    )(z_p, u_fold, h0, m_loc, m_end, m_car, dvec, lr_s, li_s)

    st_p = st_p.reshape(nt, bp, 2, gt, p).transpose(2, 1, 0, 3, 4).reshape(2, bp, g, p)
    st_s = st_s.reshape(nt, rs, nseq, 2, gt, p).transpose(3, 1, 2, 0, 4, 5).reshape(2, bs, g, p)
    return z_all, st_p[0], st_p[1], st_s[0], st_s[1]


def _glu_kernel(zf_ref, w_ref, b_ref, o_ref, w_bf, z_nat):
    j = pl.program_id(0)

    @pl.when(pl.program_id(1) == 0)
    def _():
        _cast_rows(w_ref, w_bf, 256)

    nt, rf, _ = zf_ref.shape
    tm = rf * FOLD
    for k in range(nt):
        for t in range(FOLD):
            z_nat[k, pl.ds(t, rf, stride=FOLD), :] = zf_ref[k, :, t * LANES:(t + 1) * LANES]
    z = jnp.concatenate([z_nat[k] for k in range(nt)], axis=-1)
    tn = o_ref.shape[1]
    acc = jnp.dot(z.astype(BF16), w_bf[...], preferred_element_type=F32) + b_ref[...]
    nk = tn // LANES
    for kk in range(nk):
        zt = z_nat[j * nk + kk]
        o_ref[:, kk * LANES:(kk + 1) * LANES] = (zt * _sigmoid(acc[:, kk * LANES:(kk + 1) * LANES])).astype(o_ref.dtype)
    del tm


def _glu(z_fold, w_glu, b_glu):
    nt, tf, wf = z_fold.shape
    t = tf * FOLD
    n = nt * LANES
    tn = min(512, n)
    tm = _pick_tile(t, 512, SUBLANES * FOLD)
    return pl.pallas_call(
        _glu_kernel,
        grid=(n // tn, t // tm),
        in_specs=[pl.BlockSpec((nt, tm // FOLD, wf), lambda j, i: (0, i, 0)),
                  pl.BlockSpec((n, tn), lambda j, i: (0, j)),
                  pl.BlockSpec((1, tn), lambda j, i: (0, j))],
        out_specs=pl.BlockSpec((tm, tn), lambda j, i: (i, j)),
        out_shape=jax.ShapeDtypeStruct((t, n), BF16),
        scratch_shapes=[pltpu.VMEM((n, tn), BF16), pltpu.VMEM((nt, tm, LANES), F32)],
        compiler_params=_cparams(2),
        name="s5_glu",
    )(z_fold, w_glu, b_glu.reshape(1, n))


def _hg_gates(qin, fin, lb):
    q = qin * _sigmoid(qin)
    fg = lb + (1.0 - lb) * _sigmoid(fin)
    return q, fg, jnp.log(fg), 1.0 - fg


def _cumsum_rows(logf, tri_bf):
    h1, h2, h3 = _split3(logf)
    return (jnp.dot(tri_bf, h1, preferred_element_type=F32) + jnp.dot(tri_bf, h2, preferred_element_type=F32)
            + jnp.dot(tri_bf, h3, preferred_element_type=F32))


def _hg_out(o, gin, gain):
    o = o * lax.rsqrt(jnp.mean(o * o, axis=-1, keepdims=True) + NORM_EPS) * gain
    return o * (gin * _sigmoid(gin))


def _hg_prompt_kernel(q_ref, f_ref, i_ref, g_ref, lb_ref, gain_ref, o_ref, s_ref, st):
    n = pl.program_id(2)
    c, sub = HG_CHUNK, HG_SUB
    nsub = c // sub

    @pl.when(n == 0)
    def _():
        st[...] = jnp.zeros_like(st)

    lb = lb_ref[0]
    gain = gain_ref[...]
    ri = lax.broadcasted_iota(jnp.int32, (c, c), 0)
    ci = lax.broadcasted_iota(jnp.int32, (c, c), 1)
    tri = (ri >= ci).astype(BF16)
    t_io = lax.broadcasted_iota(jnp.int32, (SUBLANES, LANES), 0)
    nt_dims = (((1,), (1,)), ((), ()))
    tn_dims = (((0,), (0,)), ((), ()))

    for cc in range(q_ref.shape[0] // c):
        rows = slice(cc * c, (cc + 1) * c)
        q, fg, logf, k = _hg_gates(q_ref[rows, :], f_ref[rows, :], lb)
        v = i_ref[rows, :]
        b = _cumsum_rows(logf, tri)
        vb = v.astype(BF16)
        s_t = st[...]
        o_parts = []
        for i in range(nsub):
            sl = slice(i * sub, (i + 1) * sub)
            b_i, q_i, k_i, v_i = b[sl], q[sl], k[sl], v[sl]
            r_i = b[i * sub - 1:i * sub] if i > 0 else jnp.zeros((1, LANES), F32)
            o_tiles = []
            for tt in range(sub // SUBLANES):
                rs = slice(tt * SUBLANES, (tt + 1) * SUBLANES)
                b_t, q_t = b_i[rs], q_i[rs]
                o_t = jnp.zeros((SUBLANES, LANES), F32)
                for s in range((tt + 1) * SUBLANES):
                    diff = b_t - b_i[s:s + 1]
                    if s > tt * SUBLANES:
                        diff = jnp.where(t_io >= s - tt * SUBLANES, diff, -jnp.inf)
                    col = jnp.sum(q_t * k_i[s:s + 1] * jnp.exp(diff), axis=-1, keepdims=True)
                    o_t = o_t + col * v_i[s:s + 1]
                o_tiles.append(o_t)
            o_i = jnp.concatenate(o_tiles, axis=0)
            if i > 0:
                qt = (q_i * jnp.exp(b_i - r_i)).astype(BF16)
                kfull = (k[0:i * sub] * jnp.exp(r_i - b[0:i * sub])).astype(BF16)
                a = lax.dot_general(qt, kfull, nt_dims, preferred_element_type=F32)
                o_i = o_i + jnp.dot(a.astype(BF16), vb[0:i * sub], preferred_element_type=F32)
            o_parts.append(o_i)
        o = jnp.concatenate(o_parts, axis=0)
        qe = (q * jnp.exp(b)).astype(BF16)
        o = o + lax.dot_general(qe, s_t.astype(BF16), nt_dims, preferred_element_type=F32)
        b_last = b[c - 1:c]
        khat = (k * jnp.exp(b_last - b)).astype(BF16)
        st[...] = s_t * jnp.exp(b_last) + lax.dot_general(vb, khat, tn_dims, preferred_element_type=F32)
        o_ref[rows, :] = _hg_out(o, g_ref[rows, :], gain).astype(o_ref.dtype)

    @pl.when(n == pl.num_programs(2) - 1)
    def _():
        s_ref[0, 0] = st[...].T


def _hg_sample_kernel(hprev_ref, q_ref, f_ref, i_ref, g_ref, lb_ref, gain_ref, s0_ref, o_ref, s_ref, *, ls):
    del hprev_ref
    nseq = s0_ref.shape[0]
    rows = nseq * ls
    lb = lb_ref[0]
    q, fg, logf, k = _hg_gates(q_ref[...], f_ref[...], lb)
    v = i_ref[...]
    ri = lax.broadcasted_iota(jnp.int32, (rows, rows), 0)
    ci = lax.broadcasted_iota(jnp.int32, (rows, rows), 1)
    tri = ((ri >= ci) & (ri // ls == ci // ls)).astype(BF16)
    b = _cumsum_rows(logf, tri)
    step = lax.broadcasted_iota(jnp.int32, (rows, LANES), 0) % ls
    o = jnp.zeros((rows, LANES), F32)
    for d in range(ls):
        if d == 0:
            ks, bs_, vs = k, b, v
        else:
            ks, bs_, vs = pltpu.roll(k, d, 0), pltpu.roll(b, d, 0), pltpu.roll(v, d, 0)
        arg = jnp.where(step >= d, b - bs_, -jnp.inf)
        col = jnp.sum(q * ks * jnp.exp(arg), axis=-1, keepdims=True)
        o = o + col * vs
    qe = (q * jnp.exp(b)).astype(BF16)
    tn_dims = (((0,), (0,)), ((), ()))
    last = ((ci == (ri // ls) * ls + ls - 1)).astype(BF16)
    b_end = _cumsum_rows(b, last)
    khat = k * jnp.exp(b_end - b)
    e_end = jnp.exp(b_end)
    seq_of_row = lax.broadcasted_iota(jnp.int32, (rows, LANES), 0) // ls
    erow = lax.broadcasted_iota(jnp.int32, (2 * SUBLANES, LANES), 0)
    ones = jnp.ones((2 * SUBLANES, LANES), BF16)
    vb = v.astype(BF16)
    for r in range(nseq):
        mine = seq_of_row == r
        s0 = s0_ref[r, 0]
        o = o + jnp.where(mine, jnp.dot(qe, s0.astype(BF16), preferred_element_type=F32), 0.0)
        kv = lax.dot_general(jnp.where(mine, khat, 0.0).astype(BF16), vb, tn_dims, preferred_element_type=F32)
        e_r = e_end[r * ls:r * ls + 1]
        e_hi = e_r.astype(BF16).astype(F32)
        emat = jnp.where(erow == 0, e_hi, jnp.where(erow == 1, e_r - e_hi, 0.0)).astype(BF16)
        decay = lax.dot_general(emat, ones, tn_dims, preferred_element_type=F32)
        s_ref[r, 0] = s0 * decay + kv
    o_ref[...] = _hg_out(o, g_ref[...], gain_ref[...]).astype(o_ref.dtype)


def _hgrn_mixer(qfig, lb, gain, s0, bp, lp, bs, ls):
    t = qfig.shape[0]
    h, dk = lb.shape
    assert dk == LANES and qfig.shape[1] == 4 * h * dk
    rb = _pick_tile(lp, 256, HG_CHUNK)
    nb = lp // rb
    lb3 = lb.reshape(h, 1, dk)
    gain2 = gain.reshape(1, dk)
    col = lambda off: pl.BlockSpec((rb, dk), lambda b, hh, n: (b * nb + n, off * h + hh))
    out_p, s_p = pl.pallas_call(
        _hg_prompt_kernel,
        grid=(bp, h, nb),
        in_specs=[col(0), col(1), col(2), col(3),
                  pl.BlockSpec((1, 1, dk), lambda b, hh, n: (hh, 0, 0)),
                  pl.BlockSpec((1, dk), lambda b, hh, n: (0, 0))],
        out_specs=[pl.BlockSpec((rb, dk), lambda b, hh, n: (b * nb + n, hh)),
                   pl.BlockSpec((1, 1, dk, dk), lambda b, hh, n: (b, hh, 0, 0))],
        out_shape=[jax.ShapeDtypeStruct((t, h * dk), BF16),
                   jax.ShapeDtypeStruct((bp, h, dk, dk), F32)],
        scratch_shapes=[pltpu.VMEM((dk, dk), F32)],
        compiler_params=_cparams(3),
        name="hgrn_prompt",
    )(qfig, qfig, qfig, qfig, lb3, gain2)

    sb = _pick_tile(bs, 8, 1)
    rows = sb * ls
    assert rows % 16 == 0 and (bp * lp) % rows == 0
    r0 = (bp * lp) // rows
    scol = lambda off: pl.BlockSpec((rows, dk), lambda i, hh: (r0 + i, off * h + hh))
    out_all, s_s = pl.pallas_call(
        functools.partial(_hg_sample_kernel, ls=ls),
        grid=(bs // sb, h),
        in_specs=[pl.BlockSpec(memory_space=pl.ANY),
                  scol(0), scol(1), scol(2), scol(3),
                  pl.BlockSpec((1, 1, dk), lambda i, hh: (hh, 0, 0)),
                  pl.BlockSpec((1, dk), lambda i, hh: (0, 0)),
                  pl.BlockSpec((sb, 1, dk, dk), lambda i, hh: (i, hh, 0, 0))],
        out_specs=[pl.BlockSpec((rows, dk), lambda i, hh: (r0 + i, hh)),
                   pl.BlockSpec((sb, 1, dk, dk), lambda i, hh: (i, hh, 0, 0))],
        out_shape=[jax.ShapeDtypeStruct((t, h * dk), BF16),
                   jax.ShapeDtypeStruct((bs, h, dk, dk), F32)],
        input_output_aliases={0: 0},
        compiler_params=_cparams(2),
        name="hgrn_sample",
    )(out_p, qfig, qfig, qfig, qfig, lb3, gain2, s0)
    return out_all, s_p, s_s


def _router_kernel(x_ref, g_ref, w_ref, b_ref, t_ref, idx_ref, gate_ref):
    x = x_ref[...]
    ms = jnp.mean(x * x, axis=-1, keepdims=True)
    tok = x * lax.rsqrt(ms + NORM_EPS) * g_ref[...]
    half = tok.shape[1] // 2
    lo = lax.bitcast_convert_type(tok[:, :half].astype(BF16).astype(F32), jnp.uint32)
    hi = lax.bitcast_convert_type(tok[:, half:].astype(BF16).astype(F32), jnp.uint32)
    t_ref[...] = (lo >> 16) | (hi & jnp.uint32(0xFFFF0000))
    t_hi = tok.astype(BF16)
    t_lo = (tok - t_hi.astype(F32)).astype(BF16)
    w = w_ref[...]
    w_hi = w.astype(BF16)
    w_lo = (w - w_hi.astype(F32)).astype(BF16)
    logits = (jnp.dot(t_hi, w_hi, preferred_element_type=F32) + jnp.dot(t_hi, w_lo, preferred_element_type=F32)
              + jnp.dot(t_lo, w_hi, preferred_element_type=F32)) + b_ref[...]
    ne = logits.shape[1]
    lane = lax.broadcasted_iota(jnp.int32, logits.shape, 1).astype(F32)
    out_lane = lax.broadcasted_iota(jnp.int32, idx_ref.shape, 1)
    idx_out = jnp.zeros(idx_ref.shape, jnp.int32)
    val_out = jnp.zeros(idx_ref.shape, F32)
    vals = []
    cur = logits
    for r in range(TOP_K):
        m = jnp.max(cur, axis=-1, keepdims=True)
        sel = jnp.min(jnp.where(cur == m, lane, float(ne)), axis=-1, keepdims=True)
        cur = jnp.where(lane == sel, -jnp.inf, cur)
        vals.append(m)
        idx_out = jnp.where(out_lane == r, sel.astype(jnp.int32), idx_out)
    denom = sum(jnp.exp(v - vals[0]) for v in vals)
    for r in range(TOP_K):
        val_out = jnp.where(out_lane == r, jnp.exp(vals[r] - vals[0]) / denom, val_out)
    idx_ref[...] = idx_out
    gate_ref[...] = val_out


def _router(x1, g_ffn, w_router, b_router):
    t, d = x1.shape
    ne = w_router.shape[1]
    tm = _pick_tile(t, 256, 16)
    return pl.pallas_call(
        _router_kernel,
        grid=(t // tm,),
        in_specs=[pl.BlockSpec((tm, d), lambda i: (i, 0)), pl.BlockSpec((1, d), lambda i: (0, 0)),
                  pl.BlockSpec((d, ne), lambda i: (0, 0)), pl.BlockSpec((1, ne), lambda i: (0, 0))],
        out_specs=[pl.BlockSpec((tm, d // 2), lambda i: (i, 0)), pl.BlockSpec((tm, LANES), lambda i: (i, 0)),
                   pl.BlockSpec((tm, LANES), lambda i: (i, 0))],
        out_shape=[jax.ShapeDtypeStruct((t, d // 2), jnp.uint32), jax.ShapeDtypeStruct((t, LANES), jnp.int32),
                   jax.ShapeDtypeStruct((t, LANES), F32)],
        compiler_params=_cparams(1),
        name="router",
    )(x1, g_ffn.reshape(1, d), w_router, b_router.reshape(1, ne))


def _row_copy(src_hbm, dst_vmem, src_row, dst_row, sem):
    return pltpu.make_async_copy(src_hbm.at[pl.ds(src_row, 1), :], dst_vmem.at[pl.ds(dst_row, 1), :], sem)


def _gather_kernel(tok_tab, src_hbm, o_ref, buf, sem):
    bm = buf.shape[1]
    i = pl.program_id(0)

    def issue(step, slot):
        for s in range(SUBLANES):
            def start(q, c, s=s):
                r = q * SUBLANES + s
                _row_copy(src_hbm, buf.at[slot], tok_tab[step * bm + r], r, sem.at[slot]).start()
                return c
            lax.fori_loop(0, bm // SUBLANES, start, 0, unroll=8)

    @pl.when(i == 0)
    def _():
        issue(0, 0)

    @pl.when(i + 1 < pl.num_programs(0))
    def _():
        issue(i + 1, (i + 1) % 2)

    slot = i % 2

    def wait(r, c):
        _row_copy(src_hbm, buf.at[slot], 0, r, sem.at[slot]).wait()
        return c

    lax.fori_loop(0, bm, wait, 0, unroll=8)
    w = buf[slot]
    half = w.shape[1]
    o_ref[:, :half] = lax.bitcast_convert_type(w << 16, F32).astype(o_ref.dtype)
    o_ref[:, half:] = lax.bitcast_convert_type(w & jnp.uint32(0xFFFF0000), F32).astype(o_ref.dtype)


def _gather_rows(src, row_tok, bm):
    p = row_tok.shape[0]
    d = 2 * src.shape[1]
    return pl.pallas_call(
        _gather_kernel,
        grid_spec=pltpu.PrefetchScalarGridSpec(
            num_scalar_prefetch=1,
            grid=(p // bm,),
            in_specs=[pl.BlockSpec(memory_space=pl.ANY)],
            out_specs=pl.BlockSpec((bm, d), lambda i, tab: (i, 0)),
            scratch_shapes=[pltpu.VMEM((2, bm, d // 2), src.dtype), pltpu.SemaphoreType.DMA((2,))],
        ),
        out_shape=jax.ShapeDtypeStruct((p, d), BF16),
        compiler_params=_cparams(1),
        name="moe_gather",
    )(row_tok, src)


def _moe_group_kernel(e_tab, blk_tab, nb_tab, x_hbm, *refs, n_w):
    del e_tab
    w_refs = refs[:n_w]
    b_refs = refs[n_w:2 * n_w]
    o_hbm = refs[2 * n_w]
    xbuf, obuf, sem_x, sem_o = refs[2 * n_w + 1:]
    g, j = pl.program_id(0), pl.program_id(1)
    nb, blk0 = nb_tab[g], blk_tab[g]
    bm, tn = obuf.shape[1], obuf.shape[2]

    def rows_copy(b):
        return pltpu.make_async_copy(x_hbm.at[pl.ds((blk0 + b) * bm, bm), :],
                                     xbuf.at[pl.ds(b * bm, bm), :], sem_x)

    @pl.when((j == 0) & (nb > 0))
    def _():
        def start(b, c):
            rows_copy(b).start()
            return c

        def wait(b, c):
            rows_copy(b).wait()
            return c

        lax.fori_loop(0, nb, start, 0)
        lax.fori_loop(0, nb, wait, 0)

    def out_copy(b, slot):
        return pltpu.make_async_copy(obuf.at[slot], o_hbm.at[pl.ds((blk0 + b) * bm, bm), pl.ds(j * tn, tn)],
                                     sem_o.at[slot])

    @pl.when(nb > 0)
    def _():
        def body(b, c):
            slot = b % 2

            @pl.when(b >= 2)
            def _():
                out_copy(b - 2, slot).wait()

            x = xbuf[pl.ds(pl.multiple_of(b * bm, bm), bm), :]
            acc = jnp.dot(x, w_refs[0][...].astype(BF16), preferred_element_type=F32) + b_refs[0][...]
            if n_w == 2:
                up = jnp.dot(x, w_refs[1][...].astype(BF16), preferred_element_type=F32) + b_refs[1][...]
                gate = jnp.minimum(acc, SWIGLU_LIMIT)
                up = jnp.clip(up, -SWIGLU_LIMIT, SWIGLU_LIMIT)
                acc = (up + 1.0) * gate * _sigmoid(SWIGLU_ALPHA * gate)
            obuf[slot] = acc.astype(obuf.dtype)
            out_copy(b, slot).start()
            return c

        lax.fori_loop(0, nb, body, 0)

        @pl.when(nb >= 2)
        def _():
            out_copy(nb - 2, nb % 2).wait()

        out_copy(nb - 1, (nb - 1) % 2).wait()


def _moe_group_tables(nb_e, n_groups):
    ne = nb_e.shape[0]
    gb = MOE_GROUP_BLOCKS
    ng_e = (nb_e + gb - 1) // gb
    g_end = jnp.cumsum(ng_e)
    g_start = g_end - ng_e
    total = g_end[-1]
    blk_start = jnp.cumsum(nb_e) - nb_e
    gi = jnp.arange(n_groups, dtype=jnp.int32)
    gc = jnp.minimum(gi, total - 1)
    e = jnp.minimum(jnp.sum((gc[:, None] >= g_end[None, :]).astype(jnp.int32), axis=1), ne - 1)
    local = gc - g_start[e]
    blk0 = blk_start[e] + gb * local
    nb = jnp.where(gi < total, jnp.minimum(gb, nb_e[e] - gb * local), 0)
    return e.astype(jnp.int32), blk0.astype(jnp.int32), nb.astype(jnp.int32)


def _moe_grouped(x, tabs, ws, bs, out_dtype, bm, tn):
    p, k = x.shape
    ne, _, n = ws[0].shape
    n_w = len(ws)
    nj = n // tn
    n_groups = tabs[0].shape[0]
    frozen = lambda g, j, e, blk, nb: (e[g], 0, jnp.where(nb[g] > 0, j, nj - 1))
    return pl.pallas_call(
        functools.partial(_moe_group_kernel, n_w=n_w),
        grid_spec=pltpu.PrefetchScalarGridSpec(
            num_scalar_prefetch=3,
            grid=(n_groups, nj),
            in_specs=[pl.BlockSpec(memory_space=pl.ANY)]
            + [pl.BlockSpec((None, k, tn), frozen)] * n_w
            + [pl.BlockSpec((None, 1, tn), frozen)] * n_w,
            out_specs=pl.BlockSpec(memory_space=pl.ANY),
            scratch_shapes=[pltpu.VMEM((MOE_GROUP_BLOCKS * bm, k), x.dtype),
                            pltpu.VMEM((2, bm, tn), out_dtype), pltpu.SemaphoreType.DMA(()), pltpu.SemaphoreType.DMA((2,))],
        ),
        out_shape=jax.ShapeDtypeStruct((p, n), out_dtype),
        compiler_params=_cparams(2),
        name="moe_up" if n_w == 2 else "moe_down",
    )(*tabs, x, *ws, *[b.reshape(ne, 1, n) for b in bs])


def _moe_experts(xs, nb_e, w_gate, b_gate, w_up, b_up, w_down, b_down, bm):
    p, d = xs.shape
    ne, _, f = w_gate.shape
    n_groups = ne + -(-(p // bm) // MOE_GROUP_BLOCKS)
    tabs = _moe_group_tables(nb_e, n_groups)
    hid = _moe_grouped(xs, tabs, (w_gate, w_up), (b_gate, b_up), BF16, bm, min(256, f))
    return _moe_grouped(hid, tabs, (w_down,), (b_down,), F32, bm, min(512, d))


def _combine_kernel(pos_tab, y_hbm, x_ref, gate_ref, g_ref, op_ref, os_ref, buf, sem, *, npb):
    tc = x_ref.shape[0]
    i = pl.program_id(0)

    def issue(step, slot):
        def start(r, c):
            for k in range(TOP_K):
                _row_copy(y_hbm, buf.at[slot, k], pos_tab[(step * tc + r) * TOP_K + k], r, sem.at[slot]).start()
            return c
        lax.fori_loop(0, tc, start, 0, unroll=2)

    @pl.when(i == 0)
    def _():
        issue(0, 0)

    @pl.when(i + 1 < pl.num_programs(0))
    def _():
        issue(i + 1, (i + 1) % 2)

    slot = i % 2

    def wait(r, c):
        for k in range(TOP_K):
            _row_copy(y_hbm, buf.at[slot, k], 0, r, sem.at[slot]).wait()
        return c

    lax.fori_loop(0, tc, wait, 0, unroll=2)
    gates = gate_ref[...]
    acc = x_ref[...]
    for k in range(TOP_K):
        acc = acc + gates[:, k:k + 1] * buf[slot, k]
    ms = jnp.mean(acc * acc, axis=-1, keepdims=True)
    out = acc * lax.rsqrt(ms + NORM_EPS) * g_ref[...]

    @pl.when(i < npb)
    def _():
        op_ref[...] = out

    @pl.when(i >= npb)
    def _():
        os_ref[...] = out


def _combine(y, pos, x1, gates, g_final, tp):
    t, d = x1.shape
    ts = t - tp
    tc = _pick_tile(math.gcd(tp, ts), 64, 8)
    npb = tp // tc
    return pl.pallas_call(
        functools.partial(_combine_kernel, npb=npb),
        grid_spec=pltpu.PrefetchScalarGridSpec(
            num_scalar_prefetch=1,
            grid=(t // tc,),
            in_specs=[pl.BlockSpec(memory_space=pl.ANY),
                      pl.BlockSpec((tc, d), lambda i, tab: (i, 0)),
                      pl.BlockSpec((tc, LANES), lambda i, tab: (i, 0)),
                      pl.BlockSpec((1, d), lambda i, tab: (0, 0))],
            out_specs=[pl.BlockSpec((tc, d), lambda i, tab: (jnp.minimum(i, npb - 1), 0)),
                       pl.BlockSpec((tc, d), lambda i, tab: (jnp.maximum(i - npb, 0), 0))],
            scratch_shapes=[pltpu.VMEM((2, TOP_K, tc, d), F32), pltpu.SemaphoreType.DMA((2,))],
        ),
        out_shape=[jax.ShapeDtypeStruct((tp, d), F32), jax.ShapeDtypeStruct((ts, d), F32)],
        compiler_params=_cparams(1),
        name="moe_combine",
    )(pos, y, x1, gates, g_final.reshape(1, d))


def kernel(x_prompt, x_sample, state_ssm_re, state_ssm_im, state_hgrn, g_mix, w_in, ssm_a_re, ssm_a_im, ssm_log_dt, ssm_b_re, ssm_b_im, ssm_c_re, ssm_c_im, ssm_d, w_glu, b_glu, hg_lb_logits, hg_o_gain, w_out, g_ffn, w_router, b_router, w_gate, b_gate, w_up, b_up, w_down, b_down, g_final):
    bp, lp, d = x_prompt.shape
    bs, ls, _ = x_sample.shape
    depth = w_in.shape[0]
    assert depth == 1
    tp, ts = bp * lp, bs * ls
    t = tp + ts
    s5w = ssm_b_re.shape[1] * ssm_b_re.shape[3]
    h, dk = hg_lb_logits.shape[1:]
    ne = w_router.shape[2]

    x0 = (x_prompt.reshape(tp, d), x_sample.reshape(ts, d))
    lower_bounds = jnp.cumsum(jax.nn.softmax(hg_lb_logits.astype(F32), axis=0), axis=0)

    hn = _rmsnorm(x0[0], x0[1], g_mix[0], BF16)
    u_fold = _matmul([hn], w_in[0], 0, s5w, fold_out=True, tm_target=1280)
    qfig = _matmul([hn], w_in[0], s5w, w_in.shape[2] - s5w, tm_target=1280)
    ssm = (ssm_a_re[0], ssm_a_im[0], ssm_log_dt[0], ssm_b_re[0], ssm_b_im[0], ssm_c_re[0], ssm_c_im[0], ssm_d[0])
    z_fold, re_p, im_p, re_s, im_s = _s5_mixer(u_fold, ssm, state_ssm_re[0], state_ssm_im[0], bp, lp, bs, ls)
    s5_out = _glu(z_fold, w_glu[0], b_glu[0])
    hg_out, s_p, s_s = _hgrn_mixer(qfig, lower_bounds[0], hg_o_gain[0], state_hgrn[0], bp, lp, bs, ls)
    x1 = _matmul([s5_out, hg_out], w_out[0], 0, d, res=x0)

    tok, idx, gates = _router(x1, g_ffn[0], w_router[0], b_router[0])
    m = t * TOP_K
    bm = MOE_BLOCK_ROWS
    flat_e = idx[:, :TOP_K].reshape(m)
    onehot = (flat_e[:, None] == jnp.arange(ne, dtype=jnp.int32)[None, :]).astype(jnp.int32)
    seen = jnp.cumsum(onehot, axis=0)
    counts = seen[-1]
    nb_e = (counts + bm - 1) // bm
    pad_end = jnp.cumsum(nb_e * bm)
    pad_start = pad_end - nb_e * bm
    pos = jnp.sum(onehot * (seen - 1 + pad_start[None, :]), axis=1).astype(jnp.int32)
    n_blocks = -(-(m + ne * (bm - 1)) // bm)
    p_rows = n_blocks * bm
    row_tok = jnp.zeros((p_rows,), jnp.int32).at[pos].set(jnp.arange(m, dtype=jnp.int32) // TOP_K)
    xs = _gather_rows(tok, row_tok, bm)
    y_rows = _moe_experts(xs, nb_e, w_gate[0], b_gate[0], w_up[0], b_up[0], w_down[0], b_down[0], bm)
    y_p, y_s = _combine(y_rows, pos, x1, gates, g_final, tp)

    y_prompt = y_p.reshape(bp, lp, d)
    y_sample = y_s.reshape(bs, ls, d)
    sd = state_ssm_re.dtype
    return (y_prompt, y_sample, re_p[None].astype(sd), im_p[None].astype(sd), s_p[None].astype(state_hgrn.dtype),
            re_s[None].astype(sd), im_s[None].astype(sd), s_s[None].astype(state_hgrn.dtype))
```

```python
import functools
import math

import jax
import jax.numpy as jnp
from jax import lax
from jax.experimental import pallas as pl
from jax.experimental.pallas import tpu as pltpu

F32 = jnp.float32
BF16 = jnp.bfloat16
TOP_K = 4
NORM_EPS = 1e-5
SWIGLU_LIMIT = 7.0
SWIGLU_ALPHA = 1.702
LANES = 128
SUBLANES = 8
FOLD = SUBLANES
HG_CHUNK = 64
HG_SUB = 16
MOE_BLOCK_ROWS = 256
MOE_GROUP_BLOCKS = 8
VMEM_LIMIT_BYTES = 56 * 1024 * 1024


def _cparams(n_axes):
    return pltpu.CompilerParams(dimension_semantics=("arbitrary",) * n_axes,
                                vmem_limit_bytes=VMEM_LIMIT_BYTES)


def _pick_tile(n, target, mult):
    best = None
    for t in range(mult, min(n, target) + 1, mult):
        if n % t == 0:
            best = t
    assert best is not None, (n, target, mult)
    return best


def _sigmoid(x):
    return 1.0 / (1.0 + jnp.exp(-x))


def _split3(x):
    h1 = x.astype(BF16)
    r1 = x - h1.astype(F32)
    h2 = r1.astype(BF16)
    r2 = r1 - h2.astype(F32)
    return h1, h2, r2.astype(BF16)


def _cast_rows(src_ref, dst_ref, chunk):
    rows = src_ref.shape[0]
    chunk = min(chunk, rows)
    assert rows % chunk == 0

    def body(r, c):
        sl = pl.ds(pl.multiple_of(r * chunk, chunk), chunk)
        dst_ref[sl, :] = src_ref[sl, :].astype(dst_ref.dtype)
        return c

    lax.fori_loop(0, rows // chunk, body, 0)


def _split_rows_specs(xp, xs, tm, width, row_axis, col_index):
    npb = xp.shape[0] // tm
    assert xp.shape[0] % tm == 0 and xs.shape[0] % tm == 0
    spec_p = pl.BlockSpec((tm, width), lambda *ids: (jnp.minimum(ids[row_axis], npb - 1), col_index(*ids)))
    spec_s = pl.BlockSpec((tm, width), lambda *ids: (jnp.maximum(ids[row_axis] - npb, 0), col_index(*ids)))
    return [spec_p, spec_s], npb


def _rmsnorm_kernel(xp_ref, xs_ref, g_ref, o_ref, *, npb):
    x = jnp.where(pl.program_id(0) < npb, xp_ref[...], xs_ref[...])
    ms = jnp.mean(x * x, axis=-1, keepdims=True)
    o_ref[...] = (x * lax.rsqrt(ms + NORM_EPS) * g_ref[...]).astype(o_ref.dtype)


def _rmsnorm(xp, xs, g, out_dtype):
    d = xp.shape[1]
    t = xp.shape[0] + xs.shape[0]
    tm = _pick_tile(math.gcd(xp.shape[0], xs.shape[0]), 256, 16)
    specs, npb = _split_rows_specs(xp, xs, tm, d, 0, lambda i: 0)
    return pl.pallas_call(
        functools.partial(_rmsnorm_kernel, npb=npb),
        grid=(t // tm,),
        in_specs=specs + [pl.BlockSpec((1, d), lambda i: (0, 0))],
        out_specs=pl.BlockSpec((tm, d), lambda i: (i, 0)),
        out_shape=jax.ShapeDtypeStruct((t, d), out_dtype),
        compiler_params=_cparams(1),
        name="rmsnorm",
    )(xp, xs, g.reshape(1, d))


def _mm_kernel(*refs, n_a, has_res, res_npb, fold_out):
    a_refs = refs[:n_a]
    w_ref = refs[n_a]
    pos = n_a + 1
    if has_res:
        resp_ref, ress_ref = refs[pos:pos + 2]
        pos += 2
    o_ref = refs[pos]
    w_bf = refs[pos + 1]

    @pl.when(pl.program_id(1) == 0)
    def _():
        _cast_rows(w_ref, w_bf, 256)

    acc = None
    k0 = 0
    for a_ref in a_refs:
        ka = a_ref.shape[1]
        part = jnp.dot(a_ref[...], w_bf[k0:k0 + ka, :], preferred_element_type=F32)
        acc = part if acc is None else acc + part
        k0 += ka
    if has_res:
        acc = acc + jnp.where(pl.program_id(1) < res_npb, resp_ref[...], ress_ref[...])
    if not fold_out:
        o_ref[...] = acc.astype(o_ref.dtype)
    else:
        slab = refs[pos + 2]
        tm = acc.shape[0]
        for k in range(acc.shape[1] // LANES):
            slab[k] = acc[:, k * LANES:(k + 1) * LANES]
        for k in range(acc.shape[1] // LANES):
            for s in range(FOLD):
                o_ref[k, :, s * LANES:(s + 1) * LANES] = slab[k, pl.ds(s, tm // FOLD, stride=FOLD), :]


def _matmul(a_list, w, col0, ncols, *, res=None, fold_out=False, out_dtype=F32, tm_target=512, tn=512):
    t = a_list[0].shape[0]
    k_total = sum(a.shape[1] for a in a_list)
    assert w.shape[0] == k_total
    tn = math.gcd(math.gcd(ncols, col0), tn)
    assert tn % LANES == 0
    rows = t if res is None else math.gcd(res[0].shape[0], res[1].shape[0])
    tm = _pick_tile(rows, tm_target, SUBLANES * FOLD if fold_out else 16)
    nj, ni = ncols // tn, t // tm
    j0 = col0 // tn
    in_specs = [pl.BlockSpec((tm, a.shape[1]), lambda j, i: (i, 0)) for a in a_list]
    in_specs.append(pl.BlockSpec((k_total, tn), lambda j, i: (0, j + j0)))
    args = list(a_list) + [w]
    res_npb = 0
    if res is not None:
        specs, res_npb = _split_rows_specs(res[0], res[1], tm, tn, 1, lambda j, i: j)
        in_specs += specs
        args += list(res)
    scratch = [pltpu.VMEM((k_total, tn), BF16)]
    if fold_out:
        nk = tn // LANES
        out_shape = jax.ShapeDtypeStruct((ncols // LANES, t // FOLD, FOLD * LANES), F32)
        out_spec = pl.BlockSpec((nk, tm // FOLD, FOLD * LANES), lambda j, i: (j, i, 0))
        scratch.append(pltpu.VMEM((nk, tm, LANES), F32))
    else:
        out_shape = jax.ShapeDtypeStruct((t, ncols), out_dtype)
        out_spec = pl.BlockSpec((tm, tn), lambda j, i: (i, j))
    return pl.pallas_call(
        functools.partial(_mm_kernel, n_a=len(a_list), has_res=res is not None, res_npb=res_npb, fold_out=fold_out),
        grid=(nj, ni),
        in_specs=in_specs,
        out_specs=out_spec,
        out_shape=out_shape,
        scratch_shapes=scratch,
        compiler_params=_cparams(2),
        name="dense_matmul",
    )(*args)


def _tile_blockdiag(x, rep, col_group):
    r, c = x.shape[-2:]
    xt = jnp.broadcast_to(x[..., None, :, :], x.shape[:-2] + (rep, r, c))
    a_idx = lax.broadcasted_iota(jnp.int32, (rep, r, c), 0)
    c_idx = lax.broadcasted_iota(jnp.int32, (rep, r, c), 2) // col_group
    return jnp.where(a_idx == c_idx, xt, 0.0).reshape(x.shape[:-2] + (rep * r, c))


def _s5_tables(a_re, a_im, log_dt, b_re, b_im, c_re, c_im, d_skip, ls):
    hi = lax.Precision.HIGHEST
    g, p, gc = b_re.shape
    gt = LANES // gc
    nt = g // gt
    cs = FOLD
    dt = jnp.exp(log_dt.astype(F32))[:, None]
    ar, ai = a_re.astype(F32) * dt, a_im.astype(F32) * dt
    mag = jnp.exp(ar)
    lr, li = mag * jnp.cos(ai), mag * jnp.sin(ai)
    den = a_re * a_re + a_im * a_im
    half = jnp.sin(0.5 * ai)
    nr, ni = jnp.expm1(ar) * jnp.cos(ai) - 2.0 * half * half, li
    fr, fi = (nr * a_re + ni * a_im) / den, (ni * a_re - nr * a_im) / den
    bbr = fr[..., None] * b_re - fi[..., None] * b_im
    bbi = fr[..., None] * b_im + fi[..., None] * b_re
    taus = jnp.arange(cs + 1, dtype=F32)[:, None, None]
    pmag = jnp.exp(taus * ar)
    pr, pi = pmag * jnp.cos(taus * ai), pmag * jnp.sin(taus * ai)
    wr = pr[..., None] * bbr - pi[..., None] * bbi
    wi = pr[..., None] * bbi + pi[..., None] * bbr
    kt = (jnp.einsum('gcp,tgpd->tgcd', c_re, wr[:cs], precision=hi)
          - jnp.einsum('gcp,tgpd->tgcd', c_im, wi[:cs], precision=hi))
    def blocks(x, rows, cols):
        x = x.reshape(x.shape[0], nt, gt, cols, rows).transpose(0, 1, 4, 2, 3)
        return _tile_blockdiag(x.reshape(x.shape[0], nt, rows, gt * cols), gt, cols).astype(BF16)

    bd = blocks(kt, gc, gc)
    zero = jnp.zeros_like(bd[0])
    m_loc = jnp.concatenate(
        [jnp.concatenate([zero] * s + [bd[tau] for tau in range(cs - s)], axis=2) for s in range(cs)], axis=1)
    er = blocks(wr[:cs], gc, p)
    ei = blocks(wi[:cs], gc, p)
    m_end = jnp.concatenate(
        [jnp.concatenate([er[cs - 1 - s], ei[cs - 1 - s]], axis=2) for s in range(cs)], axis=1)
    cr = c_re[None] * pr[1:, :, None, :] - c_im[None] * pi[1:, :, None, :]
    ci = c_re[None] * pi[1:, :, None, :] + c_im[None] * pr[1:, :, None, :]
    cbr = blocks(cr, p, gc)
    cbi = blocks(-ci, p, gc)
    m_car = jnp.concatenate([jnp.concatenate([cbr[t] for t in range(cs)], axis=2),
                             jnp.concatenate([cbi[t] for t in range(cs)], axis=2)], axis=1)
    dvec = jnp.broadcast_to(d_skip.astype(F32).reshape(nt, 1, 1, gt * gc), (nt, 1, cs, gt * gc)).reshape(nt, 1, cs * gt * gc)
    lam = lambda n: (pr[n].reshape(nt, 1, gt * p), pi[n].reshape(nt, 1, gt * p))
    return (m_loc, m_end, m_car, dvec) + lam(cs) + lam(ls)


def _gelu_tanh(x):
    return 0.5 * x * (1.0 + jnp.tanh(math.sqrt(2.0 / math.pi) * (x + 0.044715 * (x * x * x))))


def _s5_seq_kernel(u_ref, ml_ref, me_ref, mc_ref, dv_ref, lr_ref, li_ref, z_ref, st_ref, el_scr, hin_scr):
    u = u_ref[0]
    ub = u.astype(BF16)
    rows = u.shape[0]
    hw = lr_ref.shape[2]
    y = jnp.dot(ub, ml_ref[0], preferred_element_type=F32)
    el_scr[...] = jnp.dot(ub, me_ref[0], preferred_element_type=F32)
    lam_r, lam_i = lr_ref[0], li_ref[0]

    def body(n, carry):
        hr, hi_ = carry
        hin_scr[pl.ds(n, 1), 0:hw] = hr
        hin_scr[pl.ds(n, 1), hw:2 * hw] = hi_
        e = el_scr[pl.ds(n, 1), :]
        return (lam_r * hr - lam_i * hi_ + e[:, 0:hw], lam_r * hi_ + lam_i * hr + e[:, hw:2 * hw])

    zero = jnp.zeros((1, hw), F32)
    hr, hi_ = lax.fori_loop(0, rows, body, (zero, zero))
    st_ref[0, 0, :, 0:hw] = hr
    st_ref[0, 0, :, hw:2 * hw] = hi_
    y = y + jnp.dot(hin_scr[...].astype(BF16), mc_ref[0], preferred_element_type=F32) + dv_ref[0] * u
    z_ref[0] = _gelu_tanh(y)


def _s5_batch_kernel(zprev_ref, u_ref, h0_ref, ml_ref, me_ref, mc_ref, dv_ref, lr_ref, li_ref, z_ref, st_ref, *, nseq):
    del zprev_ref
    hw = lr_ref.shape[2]
    wq = ml_ref.shape[1]
    lam_r, lam_i = lr_ref[0], li_ref[0]
    for q in range(nseq):
        u = u_ref[0, :, q * wq:(q + 1) * wq]
        ub = u.astype(BF16)
        o = 2 * hw * q
        h0 = h0_ref[0, :, o:o + 2 * hw]
        y = jnp.dot(ub, ml_ref[0], preferred_element_type=F32)
        y = y + jnp.dot(h0.astype(BF16), mc_ref[0], preferred_element_type=F32) + dv_ref[0] * u
        z_ref[0, :, q * wq:(q + 1) * wq] = _gelu_tanh(y)
        el = jnp.dot(ub, me_ref[0], preferred_element_type=F32)
        hr, hi_ = h0[:, 0:hw], h0[:, hw:2 * hw]
        st_ref[0, :, o:o + hw] = lam_r * hr - lam_i * hi_ + el[:, 0:hw]
        st_ref[0, :, o + hw:o + 2 * hw] = lam_r * hi_ + lam_i * hr + el[:, hw:2 * hw]


def _s5_mixer(u_fold, ssm, h0_re, h0_im, bp, lp, bs, ls):
    a_re, a_im, log_dt, b_re, b_im, c_re, c_im, d_skip = ssm
    g, p, gc = b_re.shape
    gt = LANES // gc
    nt, tf, wf = u_fold.shape
    assert g % gt == 0 and nt == g // gt and wf == FOLD * LANES
    assert lp % FOLD == 0 and FOLD % ls == 0
    nseq = FOLD // ls
    assert bs % nseq == 0
    rp = lp // FOLD
    rs = bs // nseq
    tp_f = bp * rp
    assert tp_f % rs == 0 and tf == tp_f + rs
    hw = gt * p

    m_loc, m_end, m_car, dvec, lr_c, li_c, lr_s, li_s = _s5_tables(
        a_re, a_im, log_dt, b_re, b_im, c_re, c_im, d_skip, ls)
    tabs = (m_loc, m_end, m_car, dvec, lr_c, li_c)
    tile = lambda shape: pl.BlockSpec((1,) + shape, lambda j, b: (j, 0, 0))
    z_p, st_p = pl.pallas_call(
        _s5_seq_kernel,
        grid=(nt, bp),
        in_specs=[pl.BlockSpec((1, rp, wf), lambda j, b: (j, b, 0)),
                  tile((wf, wf)), tile((wf, 2 * hw)), tile((2 * hw, wf)),
                  tile((1, wf)), tile((1, hw)), tile((1, hw))],
        out_specs=[pl.BlockSpec((1, rp, wf), lambda j, b: (j, b, 0)),
                   pl.BlockSpec((1, 1, 1, 2 * hw), lambda j, b: (j, b, 0, 0))],
        out_shape=[jax.ShapeDtypeStruct((nt, tf, wf), F32),
                   jax.ShapeDtypeStruct((nt, bp, 1, 2 * hw), F32)],
        scratch_shapes=[pltpu.VMEM((rp, 2 * hw), F32), pltpu.VMEM((rp, 2 * hw), F32)],
        compiler_params=_cparams(2),
        name="s5_prompt",
    )(u_fold, *tabs)

    h0 = jnp.stack([h0_re.astype(F32), h0_im.astype(F32)], axis=1)
    h0 = h0.reshape(rs, nseq, 2, nt, gt * p).transpose(3, 0, 1, 2, 4).reshape(nt, rs, nseq * 2 * hw)
    sblk = tp_f // rs
    tile1 = lambda shape: pl.BlockSpec((1,) + shape, lambda j: (j, 0, 0))
    wq = ls * LANES
    z_all, st_s = pl.pallas_call(
        functools.partial(_s5_batch_kernel, nseq=nseq),
        grid=(nt,),
        in_specs=[pl.BlockSpec(memory_space=pl.ANY),
                  pl.BlockSpec((1, rs, wf), lambda j: (j, sblk, 0)),
                  tile1((rs, nseq * 2 * hw)),
                  tile1((wq, wq)),
                  pl.BlockSpec((1, wq, 2 * hw), lambda j: (j, nseq - 1, 0)),
                  tile1((2 * hw, wq)),
                  tile1((1, wq)), tile1((1, hw)), tile1((1, hw))],
        out_specs=[pl.BlockSpec((1, rs, wf), lambda j: (j, sblk, 0)),
                   tile1((rs, nseq * 2 * hw))],
        out_shape=[jax.ShapeDtypeStruct((nt, tf, wf), F32),
                   jax.ShapeDtypeStruct((nt, rs, nseq * 2 * hw), F32)],
        input_output_aliases={0: 0},
        compiler_params=_cparams(1),
        name="s5_sample",
    )(z_p, u_fold, h0, m_loc, m_end, m_car, dvec, lr_s, li_s)

    st_p = st_p.reshape(nt, bp, 2, gt, p).transpose(2, 1, 0, 3, 4).reshape(2, bp, g, p)
    st_s = st_s.reshape(nt, rs, nseq, 2, gt, p).transpose(3, 1, 2, 0, 4, 5).reshape(2, bs, g, p)
    return z_all, st_p[0], st_p[1], st_s[0], st_s[1]


def _glu_kernel(zf_ref, w_ref, b_ref, o_ref, w_bf, z_nat):
    j = pl.program_id(0)

    @pl.when(pl.program_id(1) == 0)
    def _():
        _cast_rows(w_ref, w_bf, 256)

    nt, rf, _ = zf_ref.shape
    tm = rf * FOLD
    for k in range(nt):
        for t in range(FOLD):
            z_nat[k, pl.ds(t, rf, stride=FOLD), :] = zf_ref[k, :, t * LANES:(t + 1) * LANES]
    z = jnp.concatenate([z_nat[k] for k in range(nt)], axis=-1)
    tn = o_ref.shape[1]
    acc = jnp.dot(z.astype(BF16), w_bf[...], preferred_element_type=F32) + b_ref[...]
    nk = tn // LANES
    for kk in range(nk):
        zt = z_nat[j * nk + kk]
        o_ref[:, kk * LANES:(kk + 1) * LANES] = (zt * _sigmoid(acc[:, kk * LANES:(kk + 1) * LANES])).astype(o_ref.dtype)
    del tm


def _glu(z_fold, w_glu, b_glu):
    nt, tf, wf = z_fold.shape
    t = tf * FOLD
    n = nt * LANES
    tn = min(512, n)
    tm = _pick_tile(t, 512, SUBLANES * FOLD)
    return pl.pallas_call(
        _glu_kernel,
        grid=(n // tn, t // tm),
        in_specs=[pl.BlockSpec((nt, tm // FOLD, wf), lambda j, i: (0, i, 0)),
                  pl.BlockSpec((n, tn), lambda j, i: (0, j)),
                  pl.BlockSpec((1, tn), lambda j, i: (0, j))],
        out_specs=pl.BlockSpec((tm, tn), lambda j, i: (i, j)),
        out_shape=jax.ShapeDtypeStruct((t, n), BF16),
        scratch_shapes=[pltpu.VMEM((n, tn), BF16), pltpu.VMEM((nt, tm, LANES), F32)],
        compiler_params=_cparams(2),
        name="s5_glu",
    )(z_fold, w_glu, b_glu.reshape(1, n))


def _hg_gates(qin, fin, lb):
    q = qin * _sigmoid(qin)
    fg = lb + (1.0 - lb) * _sigmoid(fin)
    return q, fg, jnp.log(fg), 1.0 - fg


def _cumsum_rows(logf, tri_bf):
    h1, h2, h3 = _split3(logf)
    return (jnp.dot(tri_bf, h1, preferred_element_type=F32) + jnp.dot(tri_bf, h2, preferred_element_type=F32)
            + jnp.dot(tri_bf, h3, preferred_element_type=F32))


def _hg_out(o, gin, gain):
    o = o * lax.rsqrt(jnp.mean(o * o, axis=-1, keepdims=True) + NORM_EPS) * gain
    return o * (gin * _sigmoid(gin))


def _hg_prompt_kernel(q_ref, f_ref, i_ref, g_ref, lb_ref, gain_ref, o_ref, s_ref, st):
    n = pl.program_id(2)
    c, sub = HG_CHUNK, HG_SUB
    nsub = c // sub

    @pl.when(n == 0)
    def _():
        st[...] = jnp.zeros_like(st)

    lb = lb_ref[0]
    gain = gain_ref[...]
    ri = lax.broadcasted_iota(jnp.int32, (c, c), 0)
    ci = lax.broadcasted_iota(jnp.int32, (c, c), 1)
    tri = (ri >= ci).astype(BF16)
    t_io = lax.broadcasted_iota(jnp.int32, (SUBLANES, LANES), 0)
    nt_dims = (((1,), (1,)), ((), ()))
    tn_dims = (((0,), (0,)), ((), ()))

    for cc in range(q_ref.shape[0] // c):
        rows = slice(cc * c, (cc + 1) * c)
        q, fg, logf, k = _hg_gates(q_ref[rows, :], f_ref[rows, :], lb)
        v = i_ref[rows, :]
        b = _cumsum_rows(logf, tri)
        vb = v.astype(BF16)
        s_t = st[...]
        o_parts = []
        for i in range(nsub):
            sl = slice(i * sub, (i + 1) * sub)
            b_i, q_i, k_i, v_i = b[sl], q[sl], k[sl], v[sl]
            r_i = b[i * sub - 1:i * sub] if i > 0 else jnp.zeros((1, LANES), F32)
            o_tiles = []
            for tt in range(sub // SUBLANES):
                rs = slice(tt * SUBLANES, (tt + 1) * SUBLANES)
                b_t, q_t = b_i[rs], q_i[rs]
                o_t = jnp.zeros((SUBLANES, LANES), F32)
                for s in range((tt + 1) * SUBLANES):
                    diff = b_t - b_i[s:s + 1]
                    if s > tt * SUBLANES:
                        diff = jnp.where(t_io >= s - tt * SUBLANES, diff, -jnp.inf)
                    col = jnp.sum(q_t * k_i[s:s + 1] * jnp.exp(diff), axis=-1, keepdims=True)
                    o_t = o_t + col * v_i[s:s + 1]
                o_tiles.append(o_t)
            o_i = jnp.concatenate(o_tiles, axis=0)
            if i > 0:
                qt = (q_i * jnp.exp(b_i - r_i)).astype(BF16)
                kfull = (k[0:i * sub] * jnp.exp(r_i - b[0:i * sub])).astype(BF16)
                a = lax.dot_general(qt, kfull, nt_dims, preferred_element_type=F32)
                o_i = o_i + jnp.dot(a.astype(BF16), vb[0:i * sub], preferred_element_type=F32)
            o_parts.append(o_i)
        o = jnp.concatenate(o_parts, axis=0)
        qe = (q * jnp.exp(b)).astype(BF16)
        o = o + lax.dot_general(qe, s_t.astype(BF16), nt_dims, preferred_element_type=F32)
        b_last = b[c - 1:c]
        khat = (k * jnp.exp(b_last - b)).astype(BF16)
        st[...] = s_t * jnp.exp(b_last) + lax.dot_general(vb, khat, tn_dims, preferred_element_type=F32)
        o_ref[rows, :] = _hg_out(o, g_ref[rows, :], gain).astype(o_ref.dtype)

    @pl.when(n == pl.num_programs(2) - 1)
    def _():
        s_ref[0, 0] = st[...].T


def _hg_sample_kernel(hprev_ref, q_ref, f_ref, i_ref, g_ref, lb_ref, gain_ref, s0_ref, o_ref, s_ref, *, ls):
    del hprev_ref
    nseq = s0_ref.shape[0]
    rows = nseq * ls
    lb = lb_ref[0]
    q, fg, logf, k = _hg_gates(q_ref[...], f_ref[...], lb)
    v = i_ref[...]
    ri = lax.broadcasted_iota(jnp.int32, (rows, rows), 0)
    ci = lax.broadcasted_iota(jnp.int32, (rows, rows), 1)
    tri = ((ri >= ci) & (ri // ls == ci // ls)).astype(BF16)
    b = _cumsum_rows(logf, tri)
    step = lax.broadcasted_iota(jnp.int32, (rows, LANES), 0) % ls
    o = jnp.zeros((rows, LANES), F32)
    for d in range(ls):
        if d == 0:
            ks, bs_, vs = k, b, v
        else:
            ks, bs_, vs = pltpu.roll(k, d, 0), pltpu.roll(b, d, 0), pltpu.roll(v, d, 0)
        arg = jnp.where(step >= d, b - bs_, -jnp.inf)
        col = jnp.sum(q * ks * jnp.exp(arg), axis=-1, keepdims=True)
        o = o + col * vs
    qe = (q * jnp.exp(b)).astype(BF16)
    tn_dims = (((0,), (0,)), ((), ()))
    last = ((ci == (ri // ls) * ls + ls - 1)).astype(BF16)
    b_end = _cumsum_rows(b, last)
    khat = k * jnp.exp(b_end - b)
    e_end = jnp.exp(b_end)
    seq_of_row = lax.broadcasted_iota(jnp.int32, (rows, LANES), 0) // ls
    erow = lax.broadcasted_iota(jnp.int32, (2 * SUBLANES, LANES), 0)
    ones = jnp.ones((2 * SUBLANES, LANES), BF16)
    vb = v.astype(BF16)
    for r in range(nseq):
        mine = seq_of_row == r
        s0 = s0_ref[r, 0]
        o = o + jnp.where(mine, jnp.dot(qe, s0.astype(BF16), preferred_element_type=F32), 0.0)
        kv = lax.dot_general(jnp.where(mine, khat, 0.0).astype(BF16), vb, tn_dims, preferred_element_type=F32)
        e_r = e_end[r * ls:r * ls + 1]
        e_hi = e_r.astype(BF16).astype(F32)
        emat = jnp.where(erow == 0, e_hi, jnp.where(erow == 1, e_r - e_hi, 0.0)).astype(BF16)
        decay = lax.dot_general(emat, ones, tn_dims, preferred_element_type=F32)
        s_ref[r, 0] = s0 * decay + kv
    o_ref[...] = _hg_out(o, g_ref[...], gain_ref[...]).astype(o_ref.dtype)


def _hgrn_mixer(qfig, lb, gain, s0, bp, lp, bs, ls):
    t = qfig.shape[0]
    h, dk = lb.shape
    assert dk == LANES and qfig.shape[1] == 4 * h * dk
    rb = _pick_tile(lp, 256, HG_CHUNK)
    nb = lp // rb
    lb3 = lb.reshape(h, 1, dk)
    gain2 = gain.reshape(1, dk)
    col = lambda off: pl.BlockSpec((rb, dk), lambda b, hh, n: (b * nb + n, off * h + hh))
    out_p, s_p = pl.pallas_call(
        _hg_prompt_kernel,
        grid=(bp, h, nb),
        in_specs=[col(0), col(1), col(2), col(3),
                  pl.BlockSpec((1, 1, dk), lambda b, hh, n: (hh, 0, 0)),
                  pl.BlockSpec((1, dk), lambda b, hh, n: (0, 0))],
        out_specs=[pl.BlockSpec((rb, dk), lambda b, hh, n: (b * nb + n, hh)),
                   pl.BlockSpec((1, 1, dk, dk), lambda b, hh, n: (b, hh, 0, 0))],
        out_shape=[jax.ShapeDtypeStruct((t, h * dk), BF16),
                   jax.ShapeDtypeStruct((bp, h, dk, dk), F32)],
        scratch_shapes=[pltpu.VMEM((dk, dk), F32)],
        compiler_params=_cparams(3),
        name="hgrn_prompt",
    )(qfig, qfig, qfig, qfig, lb3, gain2)

    sb = _pick_tile(bs, 8, 1)
    rows = sb * ls
    assert rows % 16 == 0 and (bp * lp) % rows == 0
    r0 = (bp * lp) // rows
    scol = lambda off: pl.BlockSpec((rows, dk), lambda i, hh: (r0 + i, off * h + hh))
    out_all, s_s = pl.pallas_call(
        functools.partial(_hg_sample_kernel, ls=ls),
        grid=(bs // sb, h),
        in_specs=[pl.BlockSpec(memory_space=pl.ANY),
                  scol(0), scol(1), scol(2), scol(3),
                  pl.BlockSpec((1, 1, dk), lambda i, hh: (hh, 0, 0)),
                  pl.BlockSpec((1, dk), lambda i, hh: (0, 0)),
                  pl.BlockSpec((sb, 1, dk, dk), lambda i, hh: (i, hh, 0, 0))],
        out_specs=[pl.BlockSpec((rows, dk), lambda i, hh: (r0 + i, hh)),
                   pl.BlockSpec((sb, 1, dk, dk), lambda i, hh: (i, hh, 0, 0))],
        out_shape=[jax.ShapeDtypeStruct((t, h * dk), BF16),
                   jax.ShapeDtypeStruct((bs, h, dk, dk), F32)],
        input_output_aliases={0: 0},
        compiler_params=_cparams(2),
        name="hgrn_sample",
    )(out_p, qfig, qfig, qfig, qfig, lb3, gain2, s0)
    return out_all, s_p, s_s


def _router_kernel(x_ref, g_ref, w_ref, b_ref, t_ref, idx_ref, gate_ref):
    x = x_ref[...]
    ms = jnp.mean(x * x, axis=-1, keepdims=True)
    tok = x * lax.rsqrt(ms + NORM_EPS) * g_ref[...]
    half = tok.shape[1] // 2
    lo = lax.bitcast_convert_type(tok[:, :half].astype(BF16).astype(F32), jnp.uint32)
    hi = lax.bitcast_convert_type(tok[:, half:].astype(BF16).astype(F32), jnp.uint32)
    t_ref[...] = (lo >> 16) | (hi & jnp.uint32(0xFFFF0000))
    t_hi = tok.astype(BF16)
    t_lo = (tok - t_hi.astype(F32)).astype(BF16)
    w = w_ref[...]
    w_hi = w.astype(BF16)
    w_lo = (w - w_hi.astype(F32)).astype(BF16)
    logits = (jnp.dot(t_hi, w_hi, preferred_element_type=F32) + jnp.dot(t_hi, w_lo, preferred_element_type=F32)
              + jnp.dot(t_lo, w_hi, preferred_element_type=F32)) + b_ref[...]
    ne = logits.shape[1]
    lane = lax.broadcasted_iota(jnp.int32, logits.shape, 1).astype(F32)
    out_lane = lax.broadcasted_iota(jnp.int32, idx_ref.shape, 1)
    idx_out = jnp.zeros(idx_ref.shape, jnp.int32)
    val_out = jnp.zeros(idx_ref.shape, F32)
    vals = []
    cur = logits
    for r in range(TOP_K):
        m = jnp.max(cur, axis=-1, keepdims=True)
        sel = jnp.min(jnp.where(cur == m, lane, float(ne)), axis=-1, keepdims=True)
        cur = jnp.where(lane == sel, -jnp.inf, cur)
        vals.append(m)
        idx_out = jnp.where(out_lane == r, sel.astype(jnp.int32), idx_out)
    denom = sum(jnp.exp(v - vals[0]) for v in vals)
    for r in range(TOP_K):
        val_out = jnp.where(out_lane == r, jnp.exp(vals[r] - vals[0]) / denom, val_out)
    idx_ref[...] = idx_out
    gate_ref[...] = val_out


def _router(x1, g_ffn, w_router, b_router):
    t, d = x1.shape
    ne = w_router.shape[1]
    tm = _pick_tile(t, 256, 16)
    return pl.pallas_call(
        _router_kernel,
        grid=(t // tm,),
        in_specs=[pl.BlockSpec((tm, d), lambda i: (i, 0)), pl.BlockSpec((1, d), lambda i: (0, 0)),
                  pl.BlockSpec((d, ne), lambda i: (0, 0)), pl.BlockSpec((1, ne), lambda i: (0, 0))],
        out_specs=[pl.BlockSpec((tm, d // 2), lambda i: (i, 0)), pl.BlockSpec((tm, LANES), lambda i: (i, 0)),
                   pl.BlockSpec((tm, LANES), lambda i: (i, 0))],
        out_shape=[jax.ShapeDtypeStruct((t, d // 2), jnp.uint32), jax.ShapeDtypeStruct((t, LANES), jnp.int32),
                   jax.ShapeDtypeStruct((t, LANES), F32)],
        compiler_params=_cparams(1),
        name="router",
    )(x1, g_ffn.reshape(1, d), w_router, b_router.reshape(1, ne))


def _row_copy(src_hbm, dst_vmem, src_row, dst_row, sem):
    return pltpu.make_async_copy(src_hbm.at[pl.ds(src_row, 1), :], dst_vmem.at[pl.ds(dst_row, 1), :], sem)


def _gather_kernel(tok_tab, src_hbm, o_ref, buf, sem):
    bm = buf.shape[1]
    i = pl.program_id(0)

    def issue(step, slot):
        for s in range(SUBLANES):
            def start(q, c, s=s):
                r = q * SUBLANES + s
                _row_copy(src_hbm, buf.at[slot], tok_tab[step * bm + r], r, sem.at[slot]).start(priority=s % 2)
                return c
            lax.fori_loop(0, bm // SUBLANES, start, 0, unroll=8)

    @pl.when(i == 0)
    def _():
        issue(0, 0)

    @pl.when(i + 1 < pl.num_programs(0))
    def _():
        issue(i + 1, (i + 1) % 2)

    slot = i % 2

    def wait(r, c):
        _row_copy(src_hbm, buf.at[slot], 0, r, sem.at[slot]).wait()
        return c

    lax.fori_loop(0, bm, wait, 0, unroll=8)
    w = buf[slot]
    half = w.shape[1]
    o_ref[:, :half] = lax.bitcast_convert_type(w << 16, F32).astype(o_ref.dtype)
    o_ref[:, half:] = lax.bitcast_convert_type(w & jnp.uint32(0xFFFF0000), F32).astype(o_ref.dtype)


def _gather_rows(src, row_tok, bm):
    p = row_tok.shape[0]
    d = 2 * src.shape[1]
    return pl.pallas_call(
        _gather_kernel,
        grid_spec=pltpu.PrefetchScalarGridSpec(
            num_scalar_prefetch=1,
            grid=(p // bm,),
            in_specs=[pl.BlockSpec(memory_space=pl.ANY)],
            out_specs=pl.BlockSpec((bm, d), lambda i, tab: (i, 0)),
            scratch_shapes=[pltpu.VMEM((2, bm, d // 2), src.dtype), pltpu.SemaphoreType.DMA((2,))],
        ),
        out_shape=jax.ShapeDtypeStruct((p, d), BF16),
        compiler_params=_cparams(1),
        name="moe_gather",
    )(row_tok, src)


def _moe_group_kernel(e_tab, blk_tab, nb_tab, x_hbm, *refs, n_w):
    del e_tab
    w_refs = refs[:n_w]
    b_refs = refs[n_w:2 * n_w]
    o_hbm = refs[2 * n_w]
    xbuf, obuf, sem_x, sem_o = refs[2 * n_w + 1:]
    g, j = pl.program_id(0), pl.program_id(1)
    nb, blk0 = nb_tab[g], blk_tab[g]
    bm, tn = obuf.shape[1], obuf.shape[2]

    def rows_copy(b):
        return pltpu.make_async_copy(x_hbm.at[pl.ds((blk0 + b) * bm, bm), :],
                                     xbuf.at[pl.ds(b * bm, bm), :], sem_x)

    @pl.when((j == 0) & (nb > 0))
    def _():
        def start(b, c):
            rows_copy(b).start()
            return c

        def wait(b, c):
            rows_copy(b).wait()
            return c

        lax.fori_loop(0, nb, start, 0)
        lax.fori_loop(0, nb, wait, 0)

    def out_copy(b, slot):
        return pltpu.make_async_copy(obuf.at[slot], o_hbm.at[pl.ds((blk0 + b) * bm, bm), pl.ds(j * tn, tn)],
                                     sem_o.at[slot])

    @pl.when(nb > 0)
    def _():
        def body(b, c):
            slot = b % 2

            @pl.when(b >= 2)
            def _():
                out_copy(b - 2, slot).wait()

            x = xbuf[pl.ds(pl.multiple_of(b * bm, bm), bm), :]
            acc = jnp.dot(x, w_refs[0][...].astype(BF16), preferred_element_type=F32) + b_refs[0][...]
            if n_w == 2:
                up = jnp.dot(x, w_refs[1][...].astype(BF16), preferred_element_type=F32) + b_refs[1][...]
                gate = jnp.minimum(acc, SWIGLU_LIMIT)
                up = jnp.clip(up, -SWIGLU_LIMIT, SWIGLU_LIMIT)
                acc = (up + 1.0) * gate * _sigmoid(SWIGLU_ALPHA * gate)
            obuf[slot] = acc.astype(obuf.dtype)
            out_copy(b, slot).start()
            return c

        lax.fori_loop(0, nb, body, 0)

        @pl.when(nb >= 2)
        def _():
            out_copy(nb - 2, nb % 2).wait()

        out_copy(nb - 1, (nb - 1) % 2).wait()


def _moe_group_tables(nb_e, n_groups):
    ne = nb_e.shape[0]
    gb = MOE_GROUP_BLOCKS
    ng_e = (nb_e + gb - 1) // gb
    g_end = jnp.cumsum(ng_e)
    g_start = g_end - ng_e
    total = g_end[-1]
    blk_start = jnp.cumsum(nb_e) - nb_e
    gi = jnp.arange(n_groups, dtype=jnp.int32)
    gc = jnp.minimum(gi, total - 1)
    e = jnp.minimum(jnp.sum((gc[:, None] >= g_end[None, :]).astype(jnp.int32), axis=1), ne - 1)
    local = gc - g_start[e]
    blk0 = blk_start[e] + gb * local
    nb = jnp.where(gi < total, jnp.minimum(gb, nb_e[e] - gb * local), 0)
    return e.astype(jnp.int32), blk0.astype(jnp.int32), nb.astype(jnp.int32)


def _moe_grouped(x, tabs, ws, bs, out_dtype, bm, tn):
    p, k = x.shape
    ne, _, n = ws[0].shape
    n_w = len(ws)
    nj = n // tn
    n_groups = tabs[0].shape[0]
    frozen = lambda g, j, e, blk, nb: (e[g], 0, jnp.where(nb[g] > 0, j, nj - 1))
    return pl.pallas_call(
        functools.partial(_moe_group_kernel, n_w=n_w),
        grid_spec=pltpu.PrefetchScalarGridSpec(
            num_scalar_prefetch=3,
            grid=(n_groups, nj),
            in_specs=[pl.BlockSpec(memory_space=pl.ANY)]
            + [pl.BlockSpec((None, k, tn), frozen)] * n_w
            + [pl.BlockSpec((None, 1, tn), frozen)] * n_w,
            out_specs=pl.BlockSpec(memory_space=pl.ANY),
            scratch_shapes=[pltpu.VMEM((MOE_GROUP_BLOCKS * bm, k), x.dtype),
                            pltpu.VMEM((2, bm, tn), out_dtype), pltpu.SemaphoreType.DMA(()), pltpu.SemaphoreType.DMA((2,))],
        ),
        out_shape=jax.ShapeDtypeStruct((p, n), out_dtype),
        compiler_params=_cparams(2),
        name="moe_up" if n_w == 2 else "moe_down",
    )(*tabs, x, *ws, *[b.reshape(ne, 1, n) for b in bs])


def _moe_experts(xs, nb_e, w_gate, b_gate, w_up, b_up, w_down, b_down, bm):
    p, d = xs.shape
    ne, _, f = w_gate.shape
    n_groups = ne + -(-(p // bm) // MOE_GROUP_BLOCKS)
    tabs = _moe_group_tables(nb_e, n_groups)
    hid = _moe_grouped(xs, tabs, (w_gate, w_up), (b_gate, b_up), BF16, bm, min(512, f))
    return _moe_grouped(hid, tabs, (w_down,), (b_down,), F32, bm, min(1024, d))


def _combine_kernel(pos_tab, y_hbm, x_ref, gate_ref, g_ref, op_ref, os_ref, buf, sem, *, npb):
    tc = x_ref.shape[0]
    i = pl.program_id(0)

    def issue(step, slot):
        def start(r, c):
            for k in range(TOP_K):
                _row_copy(y_hbm, buf.at[slot, k], pos_tab[(step * tc + r) * TOP_K + k], r, sem.at[slot]).start(priority=k % 2)
            return c
        lax.fori_loop(0, tc, start, 0, unroll=2)

    @pl.when(i == 0)
    def _():
        issue(0, 0)

    @pl.when(i + 1 < pl.num_programs(0))
    def _():
        issue(i + 1, (i + 1) % 2)

    slot = i % 2

    def wait(r, c):
        for k in range(TOP_K):
            _row_copy(y_hbm, buf.at[slot, k], 0, r, sem.at[slot]).wait()
        return c

    lax.fori_loop(0, tc, wait, 0, unroll=2)
    gates = gate_ref[...]
    acc = x_ref[...]
    for k in range(TOP_K):
        acc = acc + gates[:, k:k + 1] * buf[slot, k]
    ms = jnp.mean(acc * acc, axis=-1, keepdims=True)
    out = acc * lax.rsqrt(ms + NORM_EPS) * g_ref[...]

    @pl.when(i < npb)
    def _():
        op_ref[...] = out

    @pl.when(i >= npb)
    def _():
        os_ref[...] = out


def _combine(y, pos, x1, gates, g_final, tp):
    t, d = x1.shape
    ts = t - tp
    tc = _pick_tile(math.gcd(tp, ts), 64, 8)
    npb = tp // tc
    return pl.pallas_call(
        functools.partial(_combine_kernel, npb=npb),
        grid_spec=pltpu.PrefetchScalarGridSpec(
            num_scalar_prefetch=1,
            grid=(t // tc,),
            in_specs=[pl.BlockSpec(memory_space=pl.ANY),
                      pl.BlockSpec((tc, d), lambda i, tab: (i, 0)),
                      pl.BlockSpec((tc, LANES), lambda i, tab: (i, 0)),
                      pl.BlockSpec((1, d), lambda i, tab: (0, 0))],
            out_specs=[pl.BlockSpec((tc, d), lambda i, tab: (jnp.minimum(i, npb - 1), 0)),
                       pl.BlockSpec((tc, d), lambda i, tab: (jnp.maximum(i - npb, 0), 0))],
            scratch_shapes=[pltpu.VMEM((2, TOP_K, tc, d), F32), pltpu.SemaphoreType.DMA((2,))],
        ),
        out_shape=[jax.ShapeDtypeStruct((tp, d), F32), jax.ShapeDtypeStruct((ts, d), F32)],
        compiler_params=_cparams(1),
        name="moe_combine",
    )(pos, y, x1, gates, g_final.reshape(1, d))


def kernel(x_prompt, x_sample, state_ssm_re, state_ssm_im, state_hgrn, g_mix, w_in, ssm_a_re, ssm_a_im, ssm_log_dt, ssm_b_re, ssm_b_im, ssm_c_re, ssm_c_im, ssm_d, w_glu, b_glu, hg_lb_logits, hg_o_gain, w_out, g_ffn, w_router, b_router, w_gate, b_gate, w_up, b_up, w_down, b_down, g_final):
    bp, lp, d = x_prompt.shape
    bs, ls, _ = x_sample.shape
    depth = w_in.shape[0]
    assert depth == 1
    tp, ts = bp * lp, bs * ls
    t = tp + ts
    s5w = ssm_b_re.shape[1] * ssm_b_re.shape[3]
    h, dk = hg_lb_logits.shape[1:]
    ne = w_router.shape[2]

    x0 = (x_prompt.reshape(tp, d), x_sample.reshape(ts, d))
    lower_bounds = jnp.cumsum(jax.nn.softmax(hg_lb_logits.astype(F32), axis=0), axis=0)

    hn = _rmsnorm(x0[0], x0[1], g_mix[0], BF16)
    u_fold = _matmul([hn], w_in[0], 0, s5w, fold_out=True, tm_target=1280)
    qfig = _matmul([hn], w_in[0], s5w, w_in.shape[2] - s5w, tm_target=1280)
    ssm = (ssm_a_re[0], ssm_a_im[0], ssm_log_dt[0], ssm_b_re[0], ssm_b_im[0], ssm_c_re[0], ssm_c_im[0], ssm_d[0])
    z_fold, re_p, im_p, re_s, im_s = _s5_mixer(u_fold, ssm, state_ssm_re[0], state_ssm_im[0], bp, lp, bs, ls)
    s5_out = _glu(z_fold, w_glu[0], b_glu[0])
    hg_out, s_p, s_s = _hgrn_mixer(qfig, lower_bounds[0], hg_o_gain[0], state_hgrn[0], bp, lp, bs, ls)
    x1 = _matmul([s5_out, hg_out], w_out[0], 0, d, res=x0)

    tok, idx, gates = _router(x1, g_ffn[0], w_router[0], b_router[0])
    m = t * TOP_K
    bm = MOE_BLOCK_ROWS
    flat_e = idx[:, :TOP_K].reshape(m)
    onehot = (flat_e[:, None] == jnp.arange(ne, dtype=jnp.int32)[None, :]).astype(jnp.int32)
    seen = jnp.cumsum(onehot, axis=0)
    counts = seen[-1]
    nb_e = (counts + bm - 1) // bm
    pad_end = jnp.cumsum(nb_e * bm)
    pad_start = pad_end - nb_e * bm
    pos = jnp.sum(onehot * (seen - 1 + pad_start[None, :]), axis=1).astype(jnp.int32)
    n_blocks = -(-(m + ne * (bm - 1)) // bm)
    p_rows = n_blocks * bm
    row_tok = jnp.zeros((p_rows,), jnp.int32).at[pos].set(jnp.arange(m, dtype=jnp.int32) // TOP_K)
    xs = _gather_rows(tok, row_tok, bm)
    y_rows = _moe_experts(xs, nb_e, w_gate[0], b_gate[0], w_up[0], b_up[0], w_down[0], b_down[0], bm)
    y_p, y_s = _combine(y_rows, pos, x1, gates, g_final, tp)

    y_prompt = y_p.reshape(bp, lp, d)
    y_sample = y_s.reshape(bs, ls, d)
    sd = state_ssm_re.dtype
    return (y_prompt, y_sample, re_p[None].astype(sd), im_p[None].astype(sd), s_p[None].astype(state_hgrn.dtype),
            re_s[None].astype(sd), im_s[None].astype(sd), s_s[None].astype(state_hgrn.dtype))
```

```python
import functools
import math

import jax
import jax.numpy as jnp
from jax import lax
from jax.experimental import pallas as pl
from jax.experimental.pallas import tpu as pltpu

F32 = jnp.float32
BF16 = jnp.bfloat16
TOP_K = 4
NORM_EPS = 1e-5
SWIGLU_LIMIT = 7.0
SWIGLU_ALPHA = 1.702
LANES = 128
SUBLANES = 8
FOLD = SUBLANES
HG_CHUNK = 64
HG_SUB = 16
MOE_BLOCK_ROWS = 256
MOE_GROUP_BLOCKS = 8
VMEM_LIMIT_BYTES = 56 * 1024 * 1024


def _cparams(n_axes):
    return pltpu.CompilerParams(dimension_semantics=("arbitrary",) * n_axes,
                                vmem_limit_bytes=VMEM_LIMIT_BYTES)


def _pick_tile(n, target, mult):
    best = None
    for t in range(mult, min(n, target) + 1, mult):
        if n % t == 0:
            best = t
    assert best is not None, (n, target, mult)
    return best


def _sigmoid(x):
    return 1.0 / (1.0 + jnp.exp(-x))


def _split3(x):
    h1 = x.astype(BF16)
    r1 = x - h1.astype(F32)
    h2 = r1.astype(BF16)
    r2 = r1 - h2.astype(F32)
    return h1, h2, r2.astype(BF16)


def _cast_rows(src_ref, dst_ref, chunk):
    rows = src_ref.shape[0]
    chunk = min(chunk, rows)
    assert rows % chunk == 0

    def body(r, c):
        sl = pl.ds(pl.multiple_of(r * chunk, chunk), chunk)
        dst_ref[sl, :] = src_ref[sl, :].astype(dst_ref.dtype)
        return c

    lax.fori_loop(0, rows // chunk, body, 0)


def _split_rows_specs(xp, xs, tm, width, row_axis, col_index):
    npb = xp.shape[0] // tm
    assert xp.shape[0] % tm == 0 and xs.shape[0] % tm == 0
    spec_p = pl.BlockSpec((tm, width), lambda *ids: (jnp.minimum(ids[row_axis], npb - 1), col_index(*ids)))
    spec_s = pl.BlockSpec((tm, width), lambda *ids: (jnp.maximum(ids[row_axis] - npb, 0), col_index(*ids)))
    return [spec_p, spec_s], npb


def _rmsnorm_kernel(xp_ref, xs_ref, g_ref, o_ref, *, npb):
    x = jnp.where(pl.program_id(0) < npb, xp_ref[...], xs_ref[...])
    ms = jnp.mean(x * x, axis=-1, keepdims=True)
    o_ref[...] = (x * lax.rsqrt(ms + NORM_EPS) * g_ref[...]).astype(o_ref.dtype)


def _rmsnorm(xp, xs, g, out_dtype):
    d = xp.shape[1]
    t = xp.shape[0] + xs.shape[0]
    tm = _pick_tile(math.gcd(xp.shape[0], xs.shape[0]), 256, 16)
    specs, npb = _split_rows_specs(xp, xs, tm, d, 0, lambda i: 0)
    return pl.pallas_call(
        functools.partial(_rmsnorm_kernel, npb=npb),
        grid=(t // tm,),
        in_specs=specs + [pl.BlockSpec((1, d), lambda i: (0, 0))],
        out_specs=pl.BlockSpec((tm, d), lambda i: (i, 0)),
        out_shape=jax.ShapeDtypeStruct((t, d), out_dtype),
        compiler_params=_cparams(1),
        name="rmsnorm",
    )(xp, xs, g.reshape(1, d))


def _mm_kernel(*refs, n_a, has_res, res_npb, fold_out):
    a_refs = refs[:n_a]
    w_ref = refs[n_a]
    pos = n_a + 1
    if has_res:
        resp_ref, ress_ref = refs[pos:pos + 2]
        pos += 2
    o_ref = refs[pos]
    w_bf = refs[pos + 1]

    @pl.when(pl.program_id(1) == 0)
    def _():
        _cast_rows(w_ref, w_bf, 256)

    acc = None
    k0 = 0
    for a_ref in a_refs:
        ka = a_ref.shape[1]
        part = jnp.dot(a_ref[...], w_bf[k0:k0 + ka, :], preferred_element_type=F32)
        acc = part if acc is None else acc + part
        k0 += ka
    if has_res:
        acc = acc + jnp.where(pl.program_id(1) < res_npb, resp_ref[...], ress_ref[...])
    if not fold_out:
        o_ref[...] = acc.astype(o_ref.dtype)
    else:
        slab = refs[pos + 2]
        tm = acc.shape[0]
        for k in range(acc.shape[1] // LANES):
            slab[k] = acc[:, k * LANES:(k + 1) * LANES]
        for k in range(acc.shape[1] // LANES):
            for s in range(FOLD):
                o_ref[k, :, s * LANES:(s + 1) * LANES] = slab[k, pl.ds(s, tm // FOLD, stride=FOLD), :]


def _matmul(a_list, w, col0, ncols, *, res=None, fold_out=False, out_dtype=F32, tm_target=512, tn=512):
    t = a_list[0].shape[0]
    k_total = sum(a.shape[1] for a in a_list)
    assert w.shape[0] == k_total
    tn = math.gcd(math.gcd(ncols, col0), tn)
    assert tn % LANES == 0
    rows = t if res is None else math.gcd(res[0].shape[0], res[1].shape[0])
    tm = _pick_tile(rows, tm_target, SUBLANES * FOLD if fold_out else 16)
    nj, ni = ncols // tn, t // tm
    j0 = col0 // tn
    in_specs = [pl.BlockSpec((tm, a.shape[1]), lambda j, i: (i, 0)) for a in a_list]
    in_specs.append(pl.BlockSpec((k_total, tn), lambda j, i: (0, j + j0)))
    args = list(a_list) + [w]
    res_npb = 0
    if res is not None:
        specs, res_npb = _split_rows_specs(res[0], res[1], tm, tn, 1, lambda j, i: j)
        in_specs += specs
        args += list(res)
    scratch = [pltpu.VMEM((k_total, tn), BF16)]
    if fold_out:
        nk = tn // LANES
        out_shape = jax.ShapeDtypeStruct((ncols // LANES, t // FOLD, FOLD * LANES), F32)
        out_spec = pl.BlockSpec((nk, tm // FOLD, FOLD * LANES), lambda j, i: (j, i, 0))
        scratch.append(pltpu.VMEM((nk, tm, LANES), F32))
    else:
        out_shape = jax.ShapeDtypeStruct((t, ncols), out_dtype)
        out_spec = pl.BlockSpec((tm, tn), lambda j, i: (i, j))
    return pl.pallas_call(
        functools.partial(_mm_kernel, n_a=len(a_list), has_res=res is not None, res_npb=res_npb, fold_out=fold_out),
        grid=(nj, ni),
        in_specs=in_specs,
        out_specs=out_spec,
        out_shape=out_shape,
        scratch_shapes=scratch,
        compiler_params=_cparams(2),
        name="dense_matmul",
    )(*args)


def _tile_blockdiag(x, rep, col_group):
    r, c = x.shape[-2:]
    xt = jnp.broadcast_to(x[..., None, :, :], x.shape[:-2] + (rep, r, c))
    a_idx = lax.broadcasted_iota(jnp.int32, (rep, r, c), 0)
    c_idx = lax.broadcasted_iota(jnp.int32, (rep, r, c), 2) // col_group
    return jnp.where(a_idx == c_idx, xt, 0.0).reshape(x.shape[:-2] + (rep * r, c))


def _s5_tables(a_re, a_im, log_dt, b_re, b_im, c_re, c_im, d_skip, ls):
    hi = lax.Precision.HIGHEST
    g, p, gc = b_re.shape
    gt = LANES // gc
    nt = g // gt
    cs = FOLD
    dt = jnp.exp(log_dt.astype(F32))[:, None]
    ar, ai = a_re.astype(F32) * dt, a_im.astype(F32) * dt
    mag = jnp.exp(ar)
    lr, li = mag * jnp.cos(ai), mag * jnp.sin(ai)
    den = a_re * a_re + a_im * a_im
    half = jnp.sin(0.5 * ai)
    nr, ni = jnp.expm1(ar) * jnp.cos(ai) - 2.0 * half * half, li
    fr, fi = (nr * a_re + ni * a_im) / den, (ni * a_re - nr * a_im) / den
    bbr = fr[..., None] * b_re - fi[..., None] * b_im
    bbi = fr[..., None] * b_im + fi[..., None] * b_re
    taus = jnp.arange(cs + 1, dtype=F32)[:, None, None]
    pmag = jnp.exp(taus * ar)
    pr, pi = pmag * jnp.cos(taus * ai), pmag * jnp.sin(taus * ai)
    wr = pr[..., None] * bbr - pi[..., None] * bbi
    wi = pr[..., None] * bbi + pi[..., None] * bbr
    kt = (jnp.einsum('gcp,tgpd->tgcd', c_re, wr[:cs], precision=hi)
          - jnp.einsum('gcp,tgpd->tgcd', c_im, wi[:cs], precision=hi))
    def blocks(x, rows, cols):
        x = x.reshape(x.shape[0], nt, gt, cols, rows).transpose(0, 1, 4, 2, 3)
        return _tile_blockdiag(x.reshape(x.shape[0], nt, rows, gt * cols), gt, cols).astype(BF16)

    bd = blocks(kt, gc, gc)
    zero = jnp.zeros_like(bd[0])
    m_loc = jnp.concatenate(
        [jnp.concatenate([zero] * s + [bd[tau] for tau in range(cs - s)], axis=2) for s in range(cs)], axis=1)
    er = blocks(wr[:cs], gc, p)
    ei = blocks(wi[:cs], gc, p)
    m_end = jnp.concatenate(
        [jnp.concatenate([er[cs - 1 - s], ei[cs - 1 - s]], axis=2) for s in range(cs)], axis=1)
    cr = c_re[None] * pr[1:, :, None, :] - c_im[None] * pi[1:, :, None, :]
    ci = c_re[None] * pi[1:, :, None, :] + c_im[None] * pr[1:, :, None, :]
    cbr = blocks(cr, p, gc)
    cbi = blocks(-ci, p, gc)
    m_car = jnp.concatenate([jnp.concatenate([cbr[t] for t in range(cs)], axis=2),
                             jnp.concatenate([cbi[t] for t in range(cs)], axis=2)], axis=1)
    dvec = jnp.broadcast_to(d_skip.astype(F32).reshape(nt, 1, 1, gt * gc), (nt, 1, cs, gt * gc)).reshape(nt, 1, cs * gt * gc)
    lam = lambda n: (pr[n].reshape(nt, 1, gt * p), pi[n].reshape(nt, 1, gt * p))
    return (m_loc, m_end, m_car, dvec) + lam(cs) + lam(ls)


def _gelu_tanh(x):
    return 0.5 * x * (1.0 + jnp.tanh(math.sqrt(2.0 / math.pi) * (x + 0.044715 * (x * x * x))))


def _s5_seq_kernel(u_ref, ml_ref, me_ref, mc_ref, dv_ref, lr_ref, li_ref, z_ref, st_ref, el_scr, hin_scr):
    u = u_ref[0]
    ub = u.astype(BF16)
    rows = u.shape[0]
    hw = lr_ref.shape[2]
    y = jnp.dot(ub, ml_ref[0], preferred_element_type=F32)
    el_scr[...] = jnp.dot(ub, me_ref[0], preferred_element_type=F32)
    lam_r, lam_i = lr_ref[0], li_ref[0]

    def body(n, carry):
        hr, hi_ = carry
        hin_scr[pl.ds(n, 1), 0:hw] = hr
        hin_scr[pl.ds(n, 1), hw:2 * hw] = hi_
        e = el_scr[pl.ds(n, 1), :]
        return (lam_r * hr - lam_i * hi_ + e[:, 0:hw], lam_r * hi_ + lam_i * hr + e[:, hw:2 * hw])

    zero = jnp.zeros((1, hw), F32)
    hr, hi_ = lax.fori_loop(0, rows, body, (zero, zero))
    st_ref[0, 0, :, 0:hw] = hr
    st_ref[0, 0, :, hw:2 * hw] = hi_
    y = y + jnp.dot(hin_scr[...].astype(BF16), mc_ref[0], preferred_element_type=F32) + dv_ref[0] * u
    z_ref[0] = _gelu_tanh(y)


def _s5_batch_kernel(zprev_ref, u_ref, h0_ref, ml_ref, me_ref, mc_ref, dv_ref, lr_ref, li_ref, z_ref, st_ref, *, nseq):
    del zprev_ref
    hw = lr_ref.shape[2]
    wq = ml_ref.shape[1]
    lam_r, lam_i = lr_ref[0], li_ref[0]
    for q in range(nseq):
        u = u_ref[0, :, q * wq:(q + 1) * wq]
        ub = u.astype(BF16)
        o = 2 * hw * q
        h0 = h0_ref[0, :, o:o + 2 * hw]
        y = jnp.dot(ub, ml_ref[0], preferred_element_type=F32)
        y = y + jnp.dot(h0.astype(BF16), mc_ref[0], preferred_element_type=F32) + dv_ref[0] * u
        z_ref[0, :, q * wq:(q + 1) * wq] = _gelu_tanh(y)
        el = jnp.dot(ub, me_ref[0], preferred_element_type=F32)
        hr, hi_ = h0[:, 0:hw], h0[:, hw:2 * hw]
        st_ref[0, :, o:o + hw] = lam_r * hr - lam_i * hi_ + el[:, 0:hw]
        st_ref[0, :, o + hw:o + 2 * hw] = lam_r * hi_ + lam_i * hr + el[:, hw:2 * hw]


def _s5_mixer(u_fold, ssm, h0_re, h0_im, bp, lp, bs, ls):
    a_re, a_im, log_dt, b_re, b_im, c_re, c_im, d_skip = ssm
    g, p, gc = b_re.shape
    gt = LANES // gc
    nt, tf, wf = u_fold.shape
    assert g % gt == 0 and nt == g // gt and wf == FOLD * LANES
    assert lp % FOLD == 0 and FOLD % ls == 0
    nseq = FOLD // ls
    assert bs % nseq == 0
    rp = lp // FOLD
    rs = bs // nseq
    tp_f = bp * rp
    assert tp_f % rs == 0 and tf == tp_f + rs
    hw = gt * p

    m_loc, m_end, m_car, dvec, lr_c, li_c, lr_s, li_s = _s5_tables(
        a_re, a_im, log_dt, b_re, b_im, c_re, c_im, d_skip, ls)
    tabs = (m_loc, m_end, m_car, dvec, lr_c, li_c)
    tile = lambda shape: pl.BlockSpec((1,) + shape, lambda j, b: (j, 0, 0))
    z_p, st_p = pl.pallas_call(
        _s5_seq_kernel,
        grid=(nt, bp),
        in_specs=[pl.BlockSpec((1, rp, wf), lambda j, b: (j, b, 0)),
                  tile((wf, wf)), tile((wf, 2 * hw)), tile((2 * hw, wf)),
                  tile((1, wf)), tile((1, hw)), tile((1, hw))],
        out_specs=[pl.BlockSpec((1, rp, wf), lambda j, b: (j, b, 0)),
                   pl.BlockSpec((1, 1, 1, 2 * hw), lambda j, b: (j, b, 0, 0))],
        out_shape=[jax.ShapeDtypeStruct((nt, tf, wf), F32),
                   jax.ShapeDtypeStruct((nt, bp, 1, 2 * hw), F32)],
        scratch_shapes=[pltpu.VMEM((rp, 2 * hw), F32), pltpu.VMEM((rp, 2 * hw), F32)],
        compiler_params=_cparams(2),
        name="s5_prompt",
    )(u_fold, *tabs)

    h0 = jnp.stack([h0_re.astype(F32), h0_im.astype(F32)], axis=1)
    h0 = h0.reshape(rs, nseq, 2, nt, gt * p).transpose(3, 0, 1, 2, 4).reshape(nt, rs, nseq * 2 * hw)
    sblk = tp_f // rs
    tile1 = lambda shape: pl.BlockSpec((1,) + shape, lambda j: (j, 0, 0))
    wq = ls * LANES
    z_all, st_s = pl.pallas_call(
        functools.partial(_s5_batch_kernel, nseq=nseq),
        grid=(nt,),
        in_specs=[pl.BlockSpec(memory_space=pl.ANY),
                  pl.BlockSpec((1, rs, wf), lambda j: (j, sblk, 0)),
                  tile1((rs, nseq * 2 * hw)),
                  tile1((wq, wq)),
                  pl.BlockSpec((1, wq, 2 * hw), lambda j: (j, nseq - 1, 0)),
                  tile1((2 * hw, wq)),
                  tile1((1, wq)), tile1((1, hw)), tile1((1, hw))],
        out_specs=[pl.BlockSpec((1, rs, wf), lambda j: (j, sblk, 0)),
                   tile1((rs, nseq * 2 * hw))],
        out_shape=[jax.ShapeDtypeStruct((nt, tf, wf), F32),
                   jax.ShapeDtypeStruct((nt, rs, nseq * 2 * hw), F32)],
        input_output_aliases={0: 0},
        compiler_params=_cparams(1),
        name="s5_sample",
    )(z_p, u_fold, h0, m_loc, m_end, m_car, dvec, lr_s, li_s)

    st_p = st_p.reshape(nt, bp, 2, gt, p).transpose(2, 1, 0, 3, 4).reshape(2, bp, g, p)
    st_s = st_s.reshape(nt, rs, nseq, 2, gt, p).transpose(3, 1, 2, 0, 4, 5).reshape(2, bs, g, p)
    return z_all, st_p[0], st_p[1], st_s[0], st_s[1]


def _glu_kernel(zf_ref, w_ref, b_ref, o_ref, w_bf, z_nat):
    j = pl.program_id(0)

    @pl.when(pl.program_id(1) == 0)
    def _():
        _cast_rows(w_ref, w_bf, 256)

    nt, rf, _ = zf_ref.shape
    tm = rf * FOLD
    for k in range(nt):
        for t in range(FOLD):
            z_nat[k, pl.ds(t, rf, stride=FOLD), :] = zf_ref[k, :, t * LANES:(t + 1) * LANES]
    z = jnp.concatenate([z_nat[k] for k in range(nt)], axis=-1)
    tn = o_ref.shape[1]
    acc = jnp.dot(z.astype(BF16), w_bf[...], preferred_element_type=F32) + b_ref[...]
    nk = tn // LANES
    for kk in range(nk):
        zt = z_nat[j * nk + kk]
        o_ref[:, kk * LANES:(kk + 1) * LANES] = (zt * _sigmoid(acc[:, kk * LANES:(kk + 1) * LANES])).astype(o_ref.dtype)
    del tm


def _glu(z_fold, w_glu, b_glu):
    nt, tf, wf = z_fold.shape
    t = tf * FOLD
    n = nt * LANES
    tn = min(512, n)
    tm = _pick_tile(t, 512, SUBLANES * FOLD)
    return pl.pallas_call(
        _glu_kernel,
        grid=(n // tn, t // tm),
        in_specs=[pl.BlockSpec((nt, tm // FOLD, wf), lambda j, i: (0, i, 0)),
                  pl.BlockSpec((n, tn), lambda j, i: (0, j)),
                  pl.BlockSpec((1, tn), lambda j, i: (0, j))],
        out_specs=pl.BlockSpec((tm, tn), lambda j, i: (i, j)),
        out_shape=jax.ShapeDtypeStruct((t, n), BF16),
        scratch_shapes=[pltpu.VMEM((n, tn), BF16), pltpu.VMEM((nt, tm, LANES), F32)],
        compiler_params=_cparams(2),
        name="s5_glu",
    )(z_fold, w_glu, b_glu.reshape(1, n))


def _hg_gates(qin, fin, lb):
    q = qin * _sigmoid(qin)
    fg = lb + (1.0 - lb) * _sigmoid(fin)
    return q, fg, jnp.log(fg), 1.0 - fg


def _cumsum_rows(logf, tri_bf):
    h1, h2, h3 = _split3(logf)
    return (jnp.dot(tri_bf, h1, preferred_element_type=F32) + jnp.dot(tri_bf, h2, preferred_element_type=F32)
            + jnp.dot(tri_bf, h3, preferred_element_type=F32))


def _hg_out(o, gin, gain):
    o = o * lax.rsqrt(jnp.mean(o * o, axis=-1, keepdims=True) + NORM_EPS) * gain
    return o * (gin * _sigmoid(gin))


def _hg_prompt_kernel(q_ref, f_ref, i_ref, g_ref, lb_ref, gain_ref, o_ref, s_ref, st):
    n = pl.program_id(2)
    c, sub = HG_CHUNK, HG_SUB
    nsub = c // sub

    @pl.when(n == 0)
    def _():
        st[...] = jnp.zeros_like(st)

    lb = lb_ref[0]
    gain = gain_ref[...]
    ri = lax.broadcasted_iota(jnp.int32, (c, c), 0)
    ci = lax.broadcasted_iota(jnp.int32, (c, c), 1)
    tri = (ri >= ci).astype(BF16)
    t_io = lax.broadcasted_iota(jnp.int32, (SUBLANES, LANES), 0)
    nt_dims = (((1,), (1,)), ((), ()))
    tn_dims = (((0,), (0,)), ((), ()))

    for cc in range(q_ref.shape[0] // c):
        rows = slice(cc * c, (cc + 1) * c)
        q, fg, logf, k = _hg_gates(q_ref[rows, :], f_ref[rows, :], lb)
        v = i_ref[rows, :]
        b = _cumsum_rows(logf, tri)
        vb = v.astype(BF16)
        s_t = st[...]
        o_parts = []
        for i in range(nsub):
            sl = slice(i * sub, (i + 1) * sub)
            b_i, q_i, k_i, v_i = b[sl], q[sl], k[sl], v[sl]
            r_i = b[i * sub - 1:i * sub] if i > 0 else jnp.zeros((1, LANES), F32)
            o_tiles = []
            for tt in range(sub // SUBLANES):
                rs = slice(tt * SUBLANES, (tt + 1) * SUBLANES)
                b_t, q_t = b_i[rs], q_i[rs]
                o_t = jnp.zeros((SUBLANES, LANES), F32)
                for s in range((tt + 1) * SUBLANES):
                    diff = b_t - b_i[s:s + 1]
                    if s > tt * SUBLANES:
                        diff = jnp.where(t_io >= s - tt * SUBLANES, diff, -jnp.inf)
                    col = jnp.sum(q_t * k_i[s:s + 1] * jnp.exp(diff), axis=-1, keepdims=True)
                    o_t = o_t + col * v_i[s:s + 1]
                o_tiles.append(o_t)
            o_i = jnp.concatenate(o_tiles, axis=0)
            if i > 0:
                qt = (q_i * jnp.exp(b_i - r_i)).astype(BF16)
                kfull = (k[0:i * sub] * jnp.exp(r_i - b[0:i * sub])).astype(BF16)
                a = lax.dot_general(qt, kfull, nt_dims, preferred_element_type=F32)
                o_i = o_i + jnp.dot(a.astype(BF16), vb[0:i * sub], preferred_element_type=F32)
            o_parts.append(o_i)
        o = jnp.concatenate(o_parts, axis=0)
        qe = (q * jnp.exp(b)).astype(BF16)
        o = o + lax.dot_general(qe, s_t.astype(BF16), nt_dims, preferred_element_type=F32)
        b_last = b[c - 1:c]
        khat = (k * jnp.exp(b_last - b)).astype(BF16)
        st[...] = s_t * jnp.exp(b_last) + lax.dot_general(vb, khat, tn_dims, preferred_element_type=F32)
        o_ref[rows, :] = _hg_out(o, g_ref[rows, :], gain).astype(o_ref.dtype)

    @pl.when(n == pl.num_programs(2) - 1)
    def _():
        s_ref[0, 0] = st[...].T


def _hg_sample_kernel(hprev_ref, q_ref, f_ref, i_ref, g_ref, lb_ref, gain_ref, s0_ref, o_ref, s_ref, *, ls):
    del hprev_ref
    nseq = s0_ref.shape[0]
    rows = nseq * ls
    lb = lb_ref[0]
    q, fg, logf, k = _hg_gates(q_ref[...], f_ref[...], lb)
    v = i_ref[...]
    ri = lax.broadcasted_iota(jnp.int32, (rows, rows), 0)
    ci = lax.broadcasted_iota(jnp.int32, (rows, rows), 1)
    tri = ((ri >= ci) & (ri // ls == ci // ls)).astype(BF16)
    b = _cumsum_rows(logf, tri)
    step = lax.broadcasted_iota(jnp.int32, (rows, LANES), 0) % ls
    o = jnp.zeros((rows, LANES), F32)
    for d in range(ls):
        if d == 0:
            ks, bs_, vs = k, b, v
        else:
            ks, bs_, vs = pltpu.roll(k, d, 0), pltpu.roll(b, d, 0), pltpu.roll(v, d, 0)
        arg = jnp.where(step >= d, b - bs_, -jnp.inf)
        col = jnp.sum(q * ks * jnp.exp(arg), axis=-1, keepdims=True)
        o = o + col * vs
    qe = (q * jnp.exp(b)).astype(BF16)
    tn_dims = (((0,), (0,)), ((), ()))
    last = ((ci == (ri // ls) * ls + ls - 1)).astype(BF16)
    b_end = _cumsum_rows(b, last)
    khat = k * jnp.exp(b_end - b)
    e_end = jnp.exp(b_end)
    seq_of_row = lax.broadcasted_iota(jnp.int32, (rows, LANES), 0) // ls
    erow = lax.broadcasted_iota(jnp.int32, (2 * SUBLANES, LANES), 0)
    ones = jnp.ones((2 * SUBLANES, LANES), BF16)
    vb = v.astype(BF16)
    for r in range(nseq):
        mine = seq_of_row == r
        s0 = s0_ref[r, 0]
        o = o + jnp.where(mine, jnp.dot(qe, s0.astype(BF16), preferred_element_type=F32), 0.0)
        kv = lax.dot_general(jnp.where(mine, khat, 0.0).astype(BF16), vb, tn_dims, preferred_element_type=F32)
        e_r = e_end[r * ls:r * ls + 1]
        e_hi = e_r.astype(BF16).astype(F32)
        emat = jnp.where(erow == 0, e_hi, jnp.where(erow == 1, e_r - e_hi, 0.0)).astype(BF16)
        decay = lax.dot_general(emat, ones, tn_dims, preferred_element_type=F32)
        s_ref[r, 0] = s0 * decay + kv
    o_ref[...] = _hg_out(o, g_ref[...], gain_ref[...]).astype(o_ref.dtype)


def _hgrn_mixer(qfig, lb, gain, s0, bp, lp, bs, ls):
    t = qfig.shape[0]
    h, dk = lb.shape
    assert dk == LANES and qfig.shape[1] == 4 * h * dk
    rb = _pick_tile(lp, 256, HG_CHUNK)
    nb = lp // rb
    lb3 = lb.reshape(h, 1, dk)
    gain2 = gain.reshape(1, dk)
    col = lambda off: pl.BlockSpec((rb, dk), lambda b, hh, n: (b * nb + n, off * h + hh))
    out_p, s_p = pl.pallas_call(
        _hg_prompt_kernel,
        grid=(bp, h, nb),
        in_specs=[col(0), col(1), col(2), col(3),
                  pl.BlockSpec((1, 1, dk), lambda b, hh, n: (hh, 0, 0)),
                  pl.BlockSpec((1, dk), lambda b, hh, n: (0, 0))],
        out_specs=[pl.BlockSpec((rb, dk), lambda b, hh, n: (b * nb + n, hh)),
                   pl.BlockSpec((1, 1, dk, dk), lambda b, hh, n: (b, hh, 0, 0))],
        out_shape=[jax.ShapeDtypeStruct((t, h * dk), BF16),
                   jax.ShapeDtypeStruct((bp, h, dk, dk), F32)],
        scratch_shapes=[pltpu.VMEM((dk, dk), F32)],
        compiler_params=_cparams(3),
        name="hgrn_prompt",
    )(qfig, qfig, qfig, qfig, lb3, gain2)

    sb = _pick_tile(bs, 8, 1)
    rows = sb * ls
    assert rows % 16 == 0 and (bp * lp) % rows == 0
    r0 = (bp * lp) // rows
    scol = lambda off: pl.BlockSpec((rows, dk), lambda i, hh: (r0 + i, off * h + hh))
    out_all, s_s = pl.pallas_call(
        functools.partial(_hg_sample_kernel, ls=ls),
        grid=(bs // sb, h),
        in_specs=[pl.BlockSpec(memory_space=pl.ANY),
                  scol(0), scol(1), scol(2), scol(3),
                  pl.BlockSpec((1, 1, dk), lambda i, hh: (hh, 0, 0)),
                  pl.BlockSpec((1, dk), lambda i, hh: (0, 0)),
                  pl.BlockSpec((sb, 1, dk, dk), lambda i, hh: (i, hh, 0, 0))],
        out_specs=[pl.BlockSpec((rows, dk), lambda i, hh: (r0 + i, hh)),
                   pl.BlockSpec((sb, 1, dk, dk), lambda i, hh: (i, hh, 0, 0))],
        out_shape=[jax.ShapeDtypeStruct((t, h * dk), BF16),
                   jax.ShapeDtypeStruct((bs, h, dk, dk), F32)],
        input_output_aliases={0: 0},
        compiler_params=_cparams(2),
        name="hgrn_sample",
    )(out_p, qfig, qfig, qfig, qfig, lb3, gain2, s0)
    return out_all, s_p, s_s


def _router_kernel(x_ref, g_ref, w_ref, b_ref, t_ref, idx_ref, gate_ref):
    x = x_ref[...]
    ms = jnp.mean(x * x, axis=-1, keepdims=True)
    tok = x * lax.rsqrt(ms + NORM_EPS) * g_ref[...]
    half = tok.shape[1] // 2
    lo = lax.bitcast_convert_type(tok[:, :half].astype(BF16).astype(F32), jnp.uint32)
    hi = lax.bitcast_convert_type(tok[:, half:].astype(BF16).astype(F32), jnp.uint32)
    packed = (lo >> 16) | (hi & jnp.uint32(0xFFFF0000))
    nl = half // LANES
    for k in range(nl):
        t_ref[pl.ds(k, tok.shape[0], stride=nl), :] = packed[:, k * LANES:(k + 1) * LANES]
    t_hi = tok.astype(BF16)
    t_lo = (tok - t_hi.astype(F32)).astype(BF16)
    w = w_ref[...]
    w_hi = w.astype(BF16)
    w_lo = (w - w_hi.astype(F32)).astype(BF16)
    logits = (jnp.dot(t_hi, w_hi, preferred_element_type=F32) + jnp.dot(t_hi, w_lo, preferred_element_type=F32)
              + jnp.dot(t_lo, w_hi, preferred_element_type=F32)) + b_ref[...]
    ne = logits.shape[1]
    lane = lax.broadcasted_iota(jnp.int32, logits.shape, 1).astype(F32)
    out_lane = lax.broadcasted_iota(jnp.int32, idx_ref.shape, 1)
    idx_out = jnp.zeros(idx_ref.shape, jnp.int32)
    val_out = jnp.zeros(idx_ref.shape, F32)
    vals = []
    cur = logits
    for r in range(TOP_K):
        m = jnp.max(cur, axis=-1, keepdims=True)
        sel = jnp.min(jnp.where(cur == m, lane, float(ne)), axis=-1, keepdims=True)
        cur = jnp.where(lane == sel, -jnp.inf, cur)
        vals.append(m)
        idx_out = jnp.where(out_lane == r, sel.astype(jnp.int32), idx_out)
    denom = sum(jnp.exp(v - vals[0]) for v in vals)
    for r in range(TOP_K):
        val_out = jnp.where(out_lane == r, jnp.exp(vals[r] - vals[0]) / denom, val_out)
    idx_ref[...] = idx_out
    gate_ref[...] = val_out


def _router(x1, g_ffn, w_router, b_router):
    t, d = x1.shape
    ne = w_router.shape[1]
    tm = _pick_tile(t, 256, 16)
    return pl.pallas_call(
        _router_kernel,
        grid=(t // tm,),
        in_specs=[pl.BlockSpec((tm, d), lambda i: (i, 0)), pl.BlockSpec((1, d), lambda i: (0, 0)),
                  pl.BlockSpec((d, ne), lambda i: (0, 0)), pl.BlockSpec((1, ne), lambda i: (0, 0))],
        out_specs=[pl.BlockSpec((tm * (d // 2 // LANES), LANES), lambda i: (i, 0)), pl.BlockSpec((tm, LANES), lambda i: (i, 0)),
                   pl.BlockSpec((tm, LANES), lambda i: (i, 0))],
        out_shape=[jax.ShapeDtypeStruct((t * (d // 2 // LANES), LANES), jnp.uint32), jax.ShapeDtypeStruct((t, LANES), jnp.int32),
                   jax.ShapeDtypeStruct((t, LANES), F32)],
        compiler_params=_cparams(1),
        name="router",
    )(x1, g_ffn.reshape(1, d), w_router, b_router.reshape(1, ne))


def _row_copy(src_hbm, dst_vmem, src_row, dst_row, sem):
    return pltpu.make_async_copy(src_hbm.at[pl.ds(src_row, 1), :], dst_vmem.at[pl.ds(dst_row, 1), :], sem)


def _slab_copy(src_hbm, dst_vmem, src_tok, dst_row, nl, sem):
    return pltpu.make_async_copy(src_hbm.at[pl.ds(pl.multiple_of(src_tok * nl, nl), nl), :],
                                 dst_vmem.at[pl.ds(pl.multiple_of(dst_row * nl, nl), nl), :], sem)


def _gather_kernel(tok_tab, src_hbm, o_ref, buf0, buf1, sem, *, nl):
    bm = o_ref.shape[0]
    half = nl * LANES
    i = pl.program_id(0)
    bufs = (buf0, buf1)

    def issue(step, slot):
        def start(r, c):
            _slab_copy(src_hbm, bufs[slot], tok_tab[step * bm + r], r, nl, sem.at[slot]).start()
            return c
        lax.fori_loop(0, bm, start, 0, unroll=8)

    def finish(slot):
        def wait(r, c):
            _slab_copy(src_hbm, bufs[slot], 0, r, nl, sem.at[slot]).wait()
            return c
        lax.fori_loop(0, bm, wait, 0, unroll=8)
        for k in range(nl):
            w = bufs[slot][pl.ds(k, bm, stride=nl), :]
            o_ref[:, k * LANES:(k + 1) * LANES] = lax.bitcast_convert_type(w << 16, F32).astype(o_ref.dtype)
            o_ref[:, half + k * LANES:half + (k + 1) * LANES] = (
                lax.bitcast_convert_type(w & jnp.uint32(0xFFFF0000), F32).astype(o_ref.dtype))

    @pl.when(i == 0)
    def _():
        issue(0, 0)

    for slot in range(2):
        @pl.when((i % 2 == slot) & (i + 1 < pl.num_programs(0)))
        def _(slot=slot):
            issue(i + 1, 1 - slot)

        @pl.when(i % 2 == slot)
        def _(slot=slot):
            finish(slot)


def _gather_rows(src, row_tok, bm, nl):
    p = row_tok.shape[0]
    d = 2 * nl * LANES
    return pl.pallas_call(
        functools.partial(_gather_kernel, nl=nl),
        grid_spec=pltpu.PrefetchScalarGridSpec(
            num_scalar_prefetch=1,
            grid=(p // bm,),
            in_specs=[pl.BlockSpec(memory_space=pl.ANY)],
            out_specs=pl.BlockSpec((bm, d), lambda i, tab: (i, 0)),
            scratch_shapes=[pltpu.VMEM((bm * nl, LANES), src.dtype), pltpu.VMEM((bm * nl, LANES), src.dtype),
                            pltpu.SemaphoreType.DMA((2,))],
        ),
        out_shape=jax.ShapeDtypeStruct((p, d), BF16),
        compiler_params=_cparams(1),
        name="moe_gather",
    )(row_tok, src)


def _moe_group_kernel(e_tab, blk_tab, nb_tab, x_hbm, *refs, n_w):
    del e_tab
    w_refs = refs[:n_w]
    b_refs = refs[n_w:2 * n_w]
    o_hbm = refs[2 * n_w]
    xbuf, obuf, sem_x, sem_o = refs[2 * n_w + 1:]
    g, j = pl.program_id(0), pl.program_id(1)
    nb, blk0 = nb_tab[g], blk_tab[g]
    bm, tn = obuf.shape[1], obuf.shape[2]

    def rows_copy(b):
        return pltpu.make_async_copy(x_hbm.at[pl.ds((blk0 + b) * bm, bm), :],
                                     xbuf.at[pl.ds(b * bm, bm), :], sem_x)

    @pl.when((j == 0) & (nb > 0))
    def _():
        def start(b, c):
            rows_copy(b).start()
            return c

        def wait(b, c):
            rows_copy(b).wait()
            return c

        lax.fori_loop(0, nb, start, 0)
        lax.fori_loop(0, nb, wait, 0)

    def out_copy(b, slot):
        return pltpu.make_async_copy(obuf.at[slot], o_hbm.at[pl.ds((blk0 + b) * bm, bm), pl.ds(j * tn, tn)],
                                     sem_o.at[slot])

    @pl.when(nb > 0)
    def _():
        def body(b, c):
            slot = b % 2

            @pl.when(b >= 2)
            def _():
                out_copy(b - 2, slot).wait()

            x = xbuf[pl.ds(pl.multiple_of(b * bm, bm), bm), :]
            acc = jnp.dot(x, w_refs[0][...].astype(BF16), preferred_element_type=F32) + b_refs[0][...]
            if n_w == 2:
                up = jnp.dot(x, w_refs[1][...].astype(BF16), preferred_element_type=F32) + b_refs[1][...]
                gate = jnp.minimum(acc, SWIGLU_LIMIT)
                up = jnp.clip(up, -SWIGLU_LIMIT, SWIGLU_LIMIT)
                acc = (up + 1.0) * gate * _sigmoid(SWIGLU_ALPHA * gate)
            obuf[slot] = acc.astype(obuf.dtype)
            out_copy(b, slot).start()
            return c

        lax.fori_loop(0, nb, body, 0)

        @pl.when(nb >= 2)
        def _():
            out_copy(nb - 2, nb % 2).wait()

        out_copy(nb - 1, (nb - 1) % 2).wait()


def _moe_group_tables(nb_e, n_groups):
    ne = nb_e.shape[0]
    gb = MOE_GROUP_BLOCKS
    ng_e = (nb_e + gb - 1) // gb
    g_end = jnp.cumsum(ng_e)
    g_start = g_end - ng_e
    total = g_end[-1]
    blk_start = jnp.cumsum(nb_e) - nb_e
    gi = jnp.arange(n_groups, dtype=jnp.int32)
    gc = jnp.minimum(gi, total - 1)
    e = jnp.minimum(jnp.sum((gc[:, None] >= g_end[None, :]).astype(jnp.int32), axis=1), ne - 1)
    local = gc - g_start[e]
    blk0 = blk_start[e] + gb * local
    nb = jnp.where(gi < total, jnp.minimum(gb, nb_e[e] - gb * local), 0)
    return e.astype(jnp.int32), blk0.astype(jnp.int32), nb.astype(jnp.int32)


def _moe_grouped(x, tabs, ws, bs, out_dtype, bm, tn):
    p, k = x.shape
    ne, _, n = ws[0].shape
    n_w = len(ws)
    nj = n // tn
    n_groups = tabs[0].shape[0]
    frozen = lambda g, j, e, blk, nb: (e[g], 0, jnp.where(nb[g] > 0, j, nj - 1))
    return pl.pallas_call(
        functools.partial(_moe_group_kernel, n_w=n_w),
        grid_spec=pltpu.PrefetchScalarGridSpec(
            num_scalar_prefetch=3,
            grid=(n_groups, nj),
            in_specs=[pl.BlockSpec(memory_space=pl.ANY)]
            + [pl.BlockSpec((None, k, tn), frozen)] * n_w
            + [pl.BlockSpec((None, 1, tn), frozen)] * n_w,
            out_specs=pl.BlockSpec(memory_space=pl.ANY),
            scratch_shapes=[pltpu.VMEM((MOE_GROUP_BLOCKS * bm, k), x.dtype),
                            pltpu.VMEM((2, bm, tn), out_dtype), pltpu.SemaphoreType.DMA(()), pltpu.SemaphoreType.DMA((2,))],
        ),
        out_shape=jax.ShapeDtypeStruct((p, n), out_dtype),
        compiler_params=_cparams(2),
        name="moe_up" if n_w == 2 else "moe_down",
    )(*tabs, x, *ws, *[b.reshape(ne, 1, n) for b in bs])


def _moe_experts(xs, nb_e, w_gate, b_gate, w_up, b_up, w_down, b_down, bm):
    p, d = xs.shape
    ne, _, f = w_gate.shape
    n_groups = ne + -(-(p // bm) // MOE_GROUP_BLOCKS)
    tabs = _moe_group_tables(nb_e, n_groups)
    hid = _moe_grouped(xs, tabs, (w_gate, w_up), (b_gate, b_up), BF16, bm, min(512, f))
    return _moe_grouped(hid, tabs, (w_down,), (b_down,), F32, bm, min(1024, d))


def _combine_kernel(pos_tab, y_hbm, x_ref, gate_ref, g_ref, op_ref, os_ref, buf, sem, *, npb):
    tc = x_ref.shape[0]
    i = pl.program_id(0)

    def issue(step, slot):
        def start(r, c):
            for k in range(TOP_K):
                _row_copy(y_hbm, buf.at[slot, k], pos_tab[(step * tc + r) * TOP_K + k], r, sem.at[slot]).start(priority=k % 2)
            return c
        lax.fori_loop(0, tc, start, 0, unroll=2)

    @pl.when(i == 0)
    def _():
        issue(0, 0)

    @pl.when(i + 1 < pl.num_programs(0))
    def _():
        issue(i + 1, (i + 1) % 2)

    slot = i % 2

    def wait(r, c):
        for k in range(TOP_K):
            _row_copy(y_hbm, buf.at[slot, k], 0, r, sem.at[slot]).wait()
        return c

    lax.fori_loop(0, tc, wait, 0, unroll=2)
    gates = gate_ref[...]
    acc = x_ref[...]
    for k in range(TOP_K):
        acc = acc + gates[:, k:k + 1] * buf[slot, k]
    ms = jnp.mean(acc * acc, axis=-1, keepdims=True)
    out = acc * lax.rsqrt(ms + NORM_EPS) * g_ref[...]

    @pl.when(i < npb)
    def _():
        op_ref[...] = out

    @pl.when(i >= npb)
    def _():
        os_ref[...] = out


def _combine(y, pos, x1, gates, g_final, tp):
    t, d = x1.shape
    ts = t - tp
    tc = _pick_tile(math.gcd(tp, ts), 64, 8)
    npb = tp // tc
    return pl.pallas_call(
        functools.partial(_combine_kernel, npb=npb),
        grid_spec=pltpu.PrefetchScalarGridSpec(
            num_scalar_prefetch=1,
            grid=(t // tc,),
            in_specs=[pl.BlockSpec(memory_space=pl.ANY),
                      pl.BlockSpec((tc, d), lambda i, tab: (i, 0)),
                      pl.BlockSpec((tc, LANES), lambda i, tab: (i, 0)),
                      pl.BlockSpec((1, d), lambda i, tab: (0, 0))],
            out_specs=[pl.BlockSpec((tc, d), lambda i, tab: (jnp.minimum(i, npb - 1), 0)),
                       pl.BlockSpec((tc, d), lambda i, tab: (jnp.maximum(i - npb, 0), 0))],
            scratch_shapes=[pltpu.VMEM((2, TOP_K, tc, d), F32), pltpu.SemaphoreType.DMA((2,))],
        ),
        out_shape=[jax.ShapeDtypeStruct((tp, d), F32), jax.ShapeDtypeStruct((ts, d), F32)],
        compiler_params=_cparams(1),
        name="moe_combine",
    )(pos, y, x1, gates, g_final.reshape(1, d))


def kernel(x_prompt, x_sample, state_ssm_re, state_ssm_im, state_hgrn, g_mix, w_in, ssm_a_re, ssm_a_im, ssm_log_dt, ssm_b_re, ssm_b_im, ssm_c_re, ssm_c_im, ssm_d, w_glu, b_glu, hg_lb_logits, hg_o_gain, w_out, g_ffn, w_router, b_router, w_gate, b_gate, w_up, b_up, w_down, b_down, g_final):
    bp, lp, d = x_prompt.shape
    bs, ls, _ = x_sample.shape
    depth = w_in.shape[0]
    assert depth == 1
    tp, ts = bp * lp, bs * ls
    t = tp + ts
    s5w = ssm_b_re.shape[1] * ssm_b_re.shape[3]
    h, dk = hg_lb_logits.shape[1:]
    ne = w_router.shape[2]

    x0 = (x_prompt.reshape(tp, d), x_sample.reshape(ts, d))
    lower_bounds = jnp.cumsum(jax.nn.softmax(hg_lb_logits.astype(F32), axis=0), axis=0)

    hn = _rmsnorm(x0[0], x0[1], g_mix[0], BF16)
    u_fold = _matmul([hn], w_in[0], 0, s5w, fold_out=True, tm_target=1280)
    qfig = _matmul([hn], w_in[0], s5w, w_in.shape[2] - s5w, tm_target=1280)
    ssm = (ssm_a_re[0], ssm_a_im[0], ssm_log_dt[0], ssm_b_re[0], ssm_b_im[0], ssm_c_re[0], ssm_c_im[0], ssm_d[0])
    z_fold, re_p, im_p, re_s, im_s = _s5_mixer(u_fold, ssm, state_ssm_re[0], state_ssm_im[0], bp, lp, bs, ls)
    s5_out = _glu(z_fold, w_glu[0], b_glu[0])
    hg_out, s_p, s_s = _hgrn_mixer(qfig, lower_bounds[0], hg_o_gain[0], state_hgrn[0], bp, lp, bs, ls)
    x1 = _matmul([s5_out, hg_out], w_out[0], 0, d, res=x0)

    tok, idx, gates = _router(x1, g_ffn[0], w_router[0], b_router[0])
    m = t * TOP_K
    bm = MOE_BLOCK_ROWS
    flat_e = idx[:, :TOP_K].reshape(m)
    onehot = (flat_e[:, None] == jnp.arange(ne, dtype=jnp.int32)[None, :]).astype(jnp.int32)
    seen = jnp.cumsum(onehot, axis=0)
    counts = seen[-1]
    nb_e = (counts + bm - 1) // bm
    pad_end = jnp.cumsum(nb_e * bm)
    pad_start = pad_end - nb_e * bm
    pos = jnp.sum(onehot * (seen - 1 + pad_start[None, :]), axis=1).astype(jnp.int32)
    n_blocks = -(-(m + ne * (bm - 1)) // bm)
    p_rows = n_blocks * bm
    row_tok = jnp.zeros((p_rows,), jnp.int32).at[pos].set(jnp.arange(m, dtype=jnp.int32) // TOP_K)
    xs = _gather_rows(tok, row_tok, bm, d // 2 // LANES)
    y_rows = _moe_experts(xs, nb_e, w_gate[0], b_gate[0], w_up[0], b_up[0], w_down[0], b_down[0], bm)
    y_p, y_s = _combine(y_rows, pos, x1, gates, g_final, tp)

    y_prompt = y_p.reshape(bp, lp, d)
    y_sample = y_s.reshape(bs, ls, d)
    sd = state_ssm_re.dtype
    return (y_prompt, y_sample, re_p[None].astype(sd), im_p[None].astype(sd), s_p[None].astype(state_hgrn.dtype),
            re_s[None].astype(sd), im_s[None].astype(sd), s_s[None].astype(state_hgrn.dtype))
```

```python
import functools
import math

import jax
import jax.numpy as jnp
from jax import lax
from jax.experimental import pallas as pl
from jax.experimental.pallas import tpu as pltpu

F32 = jnp.float32
BF16 = jnp.bfloat16
TOP_K = 4
NORM_EPS = 1e-5
SWIGLU_LIMIT = 7.0
SWIGLU_ALPHA = 1.702
LANES = 128
SUBLANES = 8
FOLD = SUBLANES
HG_CHUNK = 64
HG_SUB = 16
MOE_BLOCK_ROWS = 256
MOE_GROUP_BLOCKS = 8
VMEM_LIMIT_BYTES = 56 * 1024 * 1024


def _cparams(n_axes):
    return pltpu.CompilerParams(dimension_semantics=("arbitrary",) * n_axes,
                                vmem_limit_bytes=VMEM_LIMIT_BYTES)


def _pick_tile(n, target, mult):
    best = None
    for t in range(mult, min(n, target) + 1, mult):
        if n % t == 0:
            best = t
    assert best is not None, (n, target, mult)
    return best


def _sigmoid(x):
    return 1.0 / (1.0 + jnp.exp(-x))


def _split3(x):
    h1 = x.astype(BF16)
    r1 = x - h1.astype(F32)
    h2 = r1.astype(BF16)
    r2 = r1 - h2.astype(F32)
    return h1, h2, r2.astype(BF16)


def _cast_rows(src_ref, dst_ref, chunk):
    rows = src_ref.shape[0]
    chunk = min(chunk, rows)
    assert rows % chunk == 0

    def body(r, c):
        sl = pl.ds(pl.multiple_of(r * chunk, chunk), chunk)
        dst_ref[sl, :] = src_ref[sl, :].astype(dst_ref.dtype)
        return c

    lax.fori_loop(0, rows // chunk, body, 0)


def _split_rows_specs(xp, xs, tm, width, row_axis, col_index):
    npb = xp.shape[0] // tm
    assert xp.shape[0] % tm == 0 and xs.shape[0] % tm == 0
    spec_p = pl.BlockSpec((tm, width), lambda *ids: (jnp.minimum(ids[row_axis], npb - 1), col_index(*ids)))
    spec_s = pl.BlockSpec((tm, width), lambda *ids: (jnp.maximum(ids[row_axis] - npb, 0), col_index(*ids)))
    return [spec_p, spec_s], npb


def _rmsnorm_kernel(xp_ref, xs_ref, g_ref, o_ref, *, npb):
    x = jnp.where(pl.program_id(0) < npb, xp_ref[...], xs_ref[...])
    ms = jnp.mean(x * x, axis=-1, keepdims=True)
    o_ref[...] = (x * lax.rsqrt(ms + NORM_EPS) * g_ref[...]).astype(o_ref.dtype)


def _rmsnorm(xp, xs, g, out_dtype):
    d = xp.shape[1]
    t = xp.shape[0] + xs.shape[0]
    tm = _pick_tile(math.gcd(xp.shape[0], xs.shape[0]), 256, 16)
    specs, npb = _split_rows_specs(xp, xs, tm, d, 0, lambda i: 0)
    return pl.pallas_call(
        functools.partial(_rmsnorm_kernel, npb=npb),
        grid=(t // tm,),
        in_specs=specs + [pl.BlockSpec((1, d), lambda i: (0, 0))],
        out_specs=pl.BlockSpec((tm, d), lambda i: (i, 0)),
        out_shape=jax.ShapeDtypeStruct((t, d), out_dtype),
        compiler_params=_cparams(1),
        name="rmsnorm",
    )(xp, xs, g.reshape(1, d))


def _mm_kernel(*refs, n_a, has_res, res_npb, fold_out):
    a_refs = refs[:n_a]
    w_ref = refs[n_a]
    pos = n_a + 1
    if has_res:
        resp_ref, ress_ref = refs[pos:pos + 2]
        pos += 2
    o_ref = refs[pos]
    w_bf = refs[pos + 1]

    @pl.when(pl.program_id(1) == 0)
    def _():
        _cast_rows(w_ref, w_bf, 256)

    acc = None
    k0 = 0
    for a_ref in a_refs:
        ka = a_ref.shape[1]
        part = jnp.dot(a_ref[...], w_bf[k0:k0 + ka, :], preferred_element_type=F32)
        acc = part if acc is None else acc + part
        k0 += ka
    if has_res:
        acc = acc + jnp.where(pl.program_id(1) < res_npb, resp_ref[...], ress_ref[...])
    if not fold_out:
        o_ref[...] = acc.astype(o_ref.dtype)
    else:
        slab = refs[pos + 2]
        tm = acc.shape[0]
        for k in range(acc.shape[1] // LANES):
            slab[k] = acc[:, k * LANES:(k + 1) * LANES]
        for k in range(acc.shape[1] // LANES):
            for s in range(FOLD):
                o_ref[k, :, s * LANES:(s + 1) * LANES] = slab[k, pl.ds(s, tm // FOLD, stride=FOLD), :]


def _matmul(a_list, w, col0, ncols, *, res=None, fold_out=False, out_dtype=F32, tm_target=512, tn=512):
    t = a_list[0].shape[0]
    k_total = sum(a.shape[1] for a in a_list)
    assert w.shape[0] == k_total
    tn = math.gcd(math.gcd(ncols, col0), tn)
    assert tn % LANES == 0
    rows = t if res is None else math.gcd(res[0].shape[0], res[1].shape[0])
    tm = _pick_tile(rows, tm_target, SUBLANES * FOLD if fold_out else 16)
    nj, ni = ncols // tn, t // tm
    j0 = col0 // tn
    in_specs = [pl.BlockSpec((tm, a.shape[1]), lambda j, i: (i, 0)) for a in a_list]
    in_specs.append(pl.BlockSpec((k_total, tn), lambda j, i: (0, j + j0)))
    args = list(a_list) + [w]
    res_npb = 0
    if res is not None:
        specs, res_npb = _split_rows_specs(res[0], res[1], tm, tn, 1, lambda j, i: j)
        in_specs += specs
        args += list(res)
    scratch = [pltpu.VMEM((k_total, tn), BF16)]
    if fold_out:
        nk = tn // LANES
        out_shape = jax.ShapeDtypeStruct((ncols // LANES, t // FOLD, FOLD * LANES), F32)
        out_spec = pl.BlockSpec((nk, tm // FOLD, FOLD * LANES), lambda j, i: (j, i, 0))
        scratch.append(pltpu.VMEM((nk, tm, LANES), F32))
    else:
        out_shape = jax.ShapeDtypeStruct((t, ncols), out_dtype)
        out_spec = pl.BlockSpec((tm, tn), lambda j, i: (i, j))
    return pl.pallas_call(
        functools.partial(_mm_kernel, n_a=len(a_list), has_res=res is not None, res_npb=res_npb, fold_out=fold_out),
        grid=(nj, ni),
        in_specs=in_specs,
        out_specs=out_spec,
        out_shape=out_shape,
        scratch_shapes=scratch,
        compiler_params=_cparams(2),
        name="dense_matmul",
    )(*args)


def _tile_blockdiag(x, rep, col_group):
    r, c = x.shape[-2:]
    xt = jnp.broadcast_to(x[..., None, :, :], x.shape[:-2] + (rep, r, c))
    a_idx = lax.broadcasted_iota(jnp.int32, (rep, r, c), 0)
    c_idx = lax.broadcasted_iota(jnp.int32, (rep, r, c), 2) // col_group
    return jnp.where(a_idx == c_idx, xt, 0.0).reshape(x.shape[:-2] + (rep * r, c))


def _s5_tables(a_re, a_im, log_dt, b_re, b_im, c_re, c_im, d_skip, ls):
    hi = lax.Precision.HIGHEST
    g, p, gc = b_re.shape
    gt = LANES // gc
    nt = g // gt
    cs = FOLD
    dt = jnp.exp(log_dt.astype(F32))[:, None]
    ar, ai = a_re.astype(F32) * dt, a_im.astype(F32) * dt
    mag = jnp.exp(ar)
    lr, li = mag * jnp.cos(ai), mag * jnp.sin(ai)
    den = a_re * a_re + a_im * a_im
    half = jnp.sin(0.5 * ai)
    nr, ni = jnp.expm1(ar) * jnp.cos(ai) - 2.0 * half * half, li
    fr, fi = (nr * a_re + ni * a_im) / den, (ni * a_re - nr * a_im) / den
    bbr = fr[..., None] * b_re - fi[..., None] * b_im
    bbi = fr[..., None] * b_im + fi[..., None] * b_re
    taus = jnp.arange(cs + 1, dtype=F32)[:, None, None]
    pmag = jnp.exp(taus * ar)
    pr, pi = pmag * jnp.cos(taus * ai), pmag * jnp.sin(taus * ai)
    wr = pr[..., None] * bbr - pi[..., None] * bbi
    wi = pr[..., None] * bbi + pi[..., None] * bbr
    kt = (jnp.einsum('gcp,tgpd->tgcd', c_re, wr[:cs], precision=hi)
          - jnp.einsum('gcp,tgpd->tgcd', c_im, wi[:cs], precision=hi))
    def blocks(x, rows, cols):
        x = x.reshape(x.shape[0], nt, gt, cols, rows).transpose(0, 1, 4, 2, 3)
        return _tile_blockdiag(x.reshape(x.shape[0], nt, rows, gt * cols), gt, cols).astype(BF16)

    bd = blocks(kt, gc, gc)
    zero = jnp.zeros_like(bd[0])
    m_loc = jnp.concatenate(
        [jnp.concatenate([zero] * s + [bd[tau] for tau in range(cs - s)], axis=2) for s in range(cs)], axis=1)
    er = blocks(wr[:cs], gc, p)
    ei = blocks(wi[:cs], gc, p)
    m_end = jnp.concatenate(
        [jnp.concatenate([er[cs - 1 - s], ei[cs - 1 - s]], axis=2) for s in range(cs)], axis=1)
    cr = c_re[None] * pr[1:, :, None, :] - c_im[None] * pi[1:, :, None, :]
    ci = c_re[None] * pi[1:, :, None, :] + c_im[None] * pr[1:, :, None, :]
    cbr = blocks(cr, p, gc)
    cbi = blocks(-ci, p, gc)
    m_car = jnp.concatenate([jnp.concatenate([cbr[t] for t in range(cs)], axis=2),
                             jnp.concatenate([cbi[t] for t in range(cs)], axis=2)], axis=1)
    dvec = jnp.broadcast_to(d_skip.astype(F32).reshape(nt, 1, 1, gt * gc), (nt, 1, cs, gt * gc)).reshape(nt, 1, cs * gt * gc)
    lam = lambda n: (pr[n].reshape(nt, 1, gt * p), pi[n].reshape(nt, 1, gt * p))
    return (m_loc, m_end, m_car, dvec) + lam(cs) + lam(ls)


def _gelu_tanh(x):
    return 0.5 * x * (1.0 + jnp.tanh(math.sqrt(2.0 / math.pi) * (x + 0.044715 * (x * x * x))))


def _s5_seq_kernel(u_ref, ml_ref, me_ref, mc_ref, dv_ref, lr_ref, li_ref, z_ref, st_ref, el_scr, hin_scr):
    u = u_ref[0]
    ub = u.astype(BF16)
    rows = u.shape[0]
    hw = lr_ref.shape[2]
    y = jnp.dot(ub, ml_ref[0], preferred_element_type=F32)
    el_scr[...] = jnp.dot(ub, me_ref[0], preferred_element_type=F32)
    lam_r, lam_i = lr_ref[0], li_ref[0]

    def body(n, carry):
        hr, hi_ = carry
        hin_scr[pl.ds(n, 1), 0:hw] = hr
        hin_scr[pl.ds(n, 1), hw:2 * hw] = hi_
        e = el_scr[pl.ds(n, 1), :]
        return (lam_r * hr - lam_i * hi_ + e[:, 0:hw], lam_r * hi_ + lam_i * hr + e[:, hw:2 * hw])

    zero = jnp.zeros((1, hw), F32)
    hr, hi_ = lax.fori_loop(0, rows, body, (zero, zero))
    st_ref[0, 0, :, 0:hw] = hr
    st_ref[0, 0, :, hw:2 * hw] = hi_
    y = y + jnp.dot(hin_scr[...].astype(BF16), mc_ref[0], preferred_element_type=F32) + dv_ref[0] * u
    z_ref[0] = _gelu_tanh(y)


def _s5_batch_kernel(zprev_ref, u_ref, h0_ref, ml_ref, me_ref, mc_ref, dv_ref, lr_ref, li_ref, z_ref, st_ref, *, nseq):
    del zprev_ref
    hw = lr_ref.shape[2]
    wq = ml_ref.shape[1]
    lam_r, lam_i = lr_ref[0], li_ref[0]
    for q in range(nseq):
        u = u_ref[0, :, q * wq:(q + 1) * wq]
        ub = u.astype(BF16)
        o = 2 * hw * q
        h0 = h0_ref[0, :, o:o + 2 * hw]
        y = jnp.dot(ub, ml_ref[0], preferred_element_type=F32)
        y = y + jnp.dot(h0.astype(BF16), mc_ref[0], preferred_element_type=F32) + dv_ref[0] * u
        z_ref[0, :, q * wq:(q + 1) * wq] = _gelu_tanh(y)
        el = jnp.dot(ub, me_ref[0], preferred_element_type=F32)
        hr, hi_ = h0[:, 0:hw], h0[:, hw:2 * hw]
        st_ref[0, :, o:o + hw] = lam_r * hr - lam_i * hi_ + el[:, 0:hw]
        st_ref[0, :, o + hw:o + 2 * hw] = lam_r * hi_ + lam_i * hr + el[:, hw:2 * hw]


def _s5_mixer(u_fold, ssm, h0_re, h0_im, bp, lp, bs, ls):
    a_re, a_im, log_dt, b_re, b_im, c_re, c_im, d_skip = ssm
    g, p, gc = b_re.shape
    gt = LANES // gc
    nt, tf, wf = u_fold.shape
    assert g % gt == 0 and nt == g // gt and wf == FOLD * LANES
    assert lp % FOLD == 0 and FOLD % ls == 0
    nseq = FOLD // ls
    assert bs % nseq == 0
    rp = lp // FOLD
    rs = bs // nseq
    tp_f = bp * rp
    assert tp_f % rs == 0 and tf == tp_f + rs
    hw = gt * p

    m_loc, m_end, m_car, dvec, lr_c, li_c, lr_s, li_s = _s5_tables(
        a_re, a_im, log_dt, b_re, b_im, c_re, c_im, d_skip, ls)
    tabs = (m_loc, m_end, m_car, dvec, lr_c, li_c)
    tile = lambda shape: pl.BlockSpec((1,) + shape, lambda j, b: (j, 0, 0))
    z_p, st_p = pl.pallas_call(
        _s5_seq_kernel,
        grid=(nt, bp),
        in_specs=[pl.BlockSpec((1, rp, wf), lambda j, b: (j, b, 0)),
                  tile((wf, wf)), tile((wf, 2 * hw)), tile((2 * hw, wf)),
                  tile((1, wf)), tile((1, hw)), tile((1, hw))],
        out_specs=[pl.BlockSpec((1, rp, wf), lambda j, b: (j, b, 0)),
                   pl.BlockSpec((1, 1, 1, 2 * hw), lambda j, b: (j, b, 0, 0))],
        out_shape=[jax.ShapeDtypeStruct((nt, tf, wf), F32),
                   jax.ShapeDtypeStruct((nt, bp, 1, 2 * hw), F32)],
        scratch_shapes=[pltpu.VMEM((rp, 2 * hw), F32), pltpu.VMEM((rp, 2 * hw), F32)],
        compiler_params=_cparams(2),
        name="s5_prompt",
    )(u_fold, *tabs)

    h0 = jnp.stack([h0_re.astype(F32), h0_im.astype(F32)], axis=1)
    h0 = h0.reshape(rs, nseq, 2, nt, gt * p).transpose(3, 0, 1, 2, 4).reshape(nt, rs, nseq * 2 * hw)
    sblk = tp_f // rs
    tile1 = lambda shape: pl.BlockSpec((1,) + shape, lambda j: (j, 0, 0))
    wq = ls * LANES
    z_all, st_s = pl.pallas_call(
        functools.partial(_s5_batch_kernel, nseq=nseq),
        grid=(nt,),
        in_specs=[pl.BlockSpec(memory_space=pl.ANY),
                  pl.BlockSpec((1, rs, wf), lambda j: (j, sblk, 0)),
                  tile1((rs, nseq * 2 * hw)),
                  tile1((wq, wq)),
                  pl.BlockSpec((1, wq, 2 * hw), lambda j: (j, nseq - 1, 0)),
                  tile1((2 * hw, wq)),
                  tile1((1, wq)), tile1((1, hw)), tile1((1, hw))],
        out_specs=[pl.BlockSpec((1, rs, wf), lambda j: (j, sblk, 0)),
                   tile1((rs, nseq * 2 * hw))],
        out_shape=[jax.ShapeDtypeStruct((nt, tf, wf), F32),
                   jax.ShapeDtypeStruct((nt, rs, nseq * 2 * hw), F32)],
        input_output_aliases={0: 0},
        compiler_params=_cparams(1),
        name="s5_sample",
    )(z_p, u_fold, h0, m_loc, m_end, m_car, dvec, lr_s, li_s)

    st_p = st_p.reshape(nt, bp, 2, gt, p).transpose(2, 1, 0, 3, 4).reshape(2, bp, g, p)
    st_s = st_s.reshape(nt, rs, nseq, 2, gt, p).transpose(3, 1, 2, 0, 4, 5).reshape(2, bs, g, p)
    return z_all, st_p[0], st_p[1], st_s[0], st_s[1]


def _glu_kernel(zf_ref, w_ref, b_ref, o_ref, w_bf, z_nat):
    j = pl.program_id(0)

    @pl.when(pl.program_id(1) == 0)
    def _():
        _cast_rows(w_ref, w_bf, 256)

    nt, rf, _ = zf_ref.shape
    tm = rf * FOLD
    for k in range(nt):
        for t in range(FOLD):
            z_nat[k, pl.ds(t, rf, stride=FOLD), :] = zf_ref[k, :, t * LANES:(t + 1) * LANES]
    z = jnp.concatenate([z_nat[k] for k in range(nt)], axis=-1)
    tn = o_ref.shape[1]
    acc = jnp.dot(z.astype(BF16), w_bf[...], preferred_element_type=F32) + b_ref[...]
    nk = tn // LANES
    for kk in range(nk):
        zt = z_nat[j * nk + kk]
        o_ref[:, kk * LANES:(kk + 1) * LANES] = (zt * _sigmoid(acc[:, kk * LANES:(kk + 1) * LANES])).astype(o_ref.dtype)
    del tm


def _glu(z_fold, w_glu, b_glu):
    nt, tf, wf = z_fold.shape
    t = tf * FOLD
    n = nt * LANES
    tn = min(512, n)
    tm = _pick_tile(t, 512, SUBLANES * FOLD)
    return pl.pallas_call(
        _glu_kernel,
        grid=(n // tn, t // tm),
        in_specs=[pl.BlockSpec((nt, tm // FOLD, wf), lambda j, i: (0, i, 0)),
                  pl.BlockSpec((n, tn), lambda j, i: (0, j)),
                  pl.BlockSpec((1, tn), lambda j, i: (0, j))],
        out_specs=pl.BlockSpec((tm, tn), lambda j, i: (i, j)),
        out_shape=jax.ShapeDtypeStruct((t, n), BF16),
        scratch_shapes=[pltpu.VMEM((n, tn), BF16), pltpu.VMEM((nt, tm, LANES), F32)],
        compiler_params=_cparams(2),
        name="s5_glu",
    )(z_fold, w_glu, b_glu.reshape(1, n))


def _hg_gates(qin, fin, lb):
    q = qin * _sigmoid(qin)
    fg = lb + (1.0 - lb) * _sigmoid(fin)
    return q, fg, jnp.log(fg), 1.0 - fg


def _cumsum_rows(logf, tri_bf):
    h1, h2, h3 = _split3(logf)
    return (jnp.dot(tri_bf, h1, preferred_element_type=F32) + jnp.dot(tri_bf, h2, preferred_element_type=F32)
            + jnp.dot(tri_bf, h3, preferred_element_type=F32))


def _hg_out(o, gin, gain):
    o = o * lax.rsqrt(jnp.mean(o * o, axis=-1, keepdims=True) + NORM_EPS) * gain
    return o * (gin * _sigmoid(gin))


def _hg_prompt_kernel(q_ref, f_ref, i_ref, g_ref, lb_ref, gain_ref, o_ref, s_ref, st):
    n = pl.program_id(2)
    c, sub = HG_CHUNK, HG_SUB
    nsub = c // sub

    @pl.when(n == 0)
    def _():
        st[...] = jnp.zeros_like(st)

    lb = lb_ref[0]
    gain = gain_ref[...]
    ri = lax.broadcasted_iota(jnp.int32, (c, c), 0)
    ci = lax.broadcasted_iota(jnp.int32, (c, c), 1)
    tri = (ri >= ci).astype(BF16)
    t_io = lax.broadcasted_iota(jnp.int32, (SUBLANES, LANES), 0)
    nt_dims = (((1,), (1,)), ((), ()))
    tn_dims = (((0,), (0,)), ((), ()))

    for cc in range(q_ref.shape[0] // c):
        rows = slice(cc * c, (cc + 1) * c)
        q, fg, logf, k = _hg_gates(q_ref[rows, :], f_ref[rows, :], lb)
        v = i_ref[rows, :]
        b = _cumsum_rows(logf, tri)
        vb = v.astype(BF16)
        s_t = st[...]
        o_parts = []
        for i in range(nsub):
            sl = slice(i * sub, (i + 1) * sub)
            b_i, q_i, k_i, v_i = b[sl], q[sl], k[sl], v[sl]
            r_i = b[i * sub - 1:i * sub] if i > 0 else jnp.zeros((1, LANES), F32)
            o_tiles = []
            for tt in range(sub // SUBLANES):
                rs = slice(tt * SUBLANES, (tt + 1) * SUBLANES)
                b_t, q_t = b_i[rs], q_i[rs]
                o_t = jnp.zeros((SUBLANES, LANES), F32)
                for s in range((tt + 1) * SUBLANES):
                    diff = b_t - b_i[s:s + 1]
                    if s > tt * SUBLANES:
                        diff = jnp.where(t_io >= s - tt * SUBLANES, diff, -jnp.inf)
                    col = jnp.sum(q_t * k_i[s:s + 1] * jnp.exp(diff), axis=-1, keepdims=True)
                    o_t = o_t + col * v_i[s:s + 1]
                o_tiles.append(o_t)
            o_i = jnp.concatenate(o_tiles, axis=0)
            if i > 0:
                qt = (q_i * jnp.exp(b_i - r_i)).astype(BF16)
                kfull = (k[0:i * sub] * jnp.exp(r_i - b[0:i * sub])).astype(BF16)
                a = lax.dot_general(qt, kfull, nt_dims, preferred_element_type=F32)
                o_i = o_i + jnp.dot(a.astype(BF16), vb[0:i * sub], preferred_element_type=F32)
            o_parts.append(o_i)
        o = jnp.concatenate(o_parts, axis=0)
        qe = (q * jnp.exp(b)).astype(BF16)
        o = o + lax.dot_general(qe, s_t.astype(BF16), nt_dims, preferred_element_type=F32)
        b_last = b[c - 1:c]
        khat = (k * jnp.exp(b_last - b)).astype(BF16)
        st[...] = s_t * jnp.exp(b_last) + lax.dot_general(vb, khat, tn_dims, preferred_element_type=F32)
        o_ref[rows, :] = _hg_out(o, g_ref[rows, :], gain).astype(o_ref.dtype)

    @pl.when(n == pl.num_programs(2) - 1)
    def _():
        s_ref[0, 0] = st[...].T


def _hg_sample_kernel(hprev_ref, q_ref, f_ref, i_ref, g_ref, lb_ref, gain_ref, s0_ref, o_ref, s_ref, *, ls):
    del hprev_ref
    nseq = s0_ref.shape[0]
    rows = nseq * ls
    lb = lb_ref[0]
    q, fg, logf, k = _hg_gates(q_ref[...], f_ref[...], lb)
    v = i_ref[...]
    ri = lax.broadcasted_iota(jnp.int32, (rows, rows), 0)
    ci = lax.broadcasted_iota(jnp.int32, (rows, rows), 1)
    tri = ((ri >= ci) & (ri // ls == ci // ls)).astype(BF16)
    b = _cumsum_rows(logf, tri)
    step = lax.broadcasted_iota(jnp.int32, (rows, LANES), 0) % ls
    o = jnp.zeros((rows, LANES), F32)
    for d in range(ls):
        if d == 0:
            ks, bs_, vs = k, b, v
        else:
            ks, bs_, vs = pltpu.roll(k, d, 0), pltpu.roll(b, d, 0), pltpu.roll(v, d, 0)
        arg = jnp.where(step >= d, b - bs_, -jnp.inf)
        col = jnp.sum(q * ks * jnp.exp(arg), axis=-1, keepdims=True)
        o = o + col * vs
    qe = (q * jnp.exp(b)).astype(BF16)
    tn_dims = (((0,), (0,)), ((), ()))
    last = ((ci == (ri // ls) * ls + ls - 1)).astype(BF16)
    b_end = _cumsum_rows(b, last)
    khat = k * jnp.exp(b_end - b)
    e_end = jnp.exp(b_end)
    seq_of_row = lax.broadcasted_iota(jnp.int32, (rows, LANES), 0) // ls
    erow = lax.broadcasted_iota(jnp.int32, (2 * SUBLANES, LANES), 0)
    ones = jnp.ones((2 * SUBLANES, LANES), BF16)
    vb = v.astype(BF16)
    for r in range(nseq):
        mine = seq_of_row == r
        s0 = s0_ref[r, 0]
        o = o + jnp.where(mine, jnp.dot(qe, s0.astype(BF16), preferred_element_type=F32), 0.0)
        kv = lax.dot_general(jnp.where(mine, khat, 0.0).astype(BF16), vb, tn_dims, preferred_element_type=F32)
        e_r = e_end[r * ls:r * ls + 1]
        e_hi = e_r.astype(BF16).astype(F32)
        emat = jnp.where(erow == 0, e_hi, jnp.where(erow == 1, e_r - e_hi, 0.0)).astype(BF16)
        decay = lax.dot_general(emat, ones, tn_dims, preferred_element_type=F32)
        s_ref[r, 0] = s0 * decay + kv
    o_ref[...] = _hg_out(o, g_ref[...], gain_ref[...]).astype(o_ref.dtype)


def _hgrn_mixer(qfig, lb, gain, s0, bp, lp, bs, ls):
    t = qfig.shape[0]
    h, dk = lb.shape
    assert dk == LANES and qfig.shape[1] == 4 * h * dk
    rb = _pick_tile(lp, 256, HG_CHUNK)
    nb = lp // rb
    lb3 = lb.reshape(h, 1, dk)
    gain2 = gain.reshape(1, dk)
    col = lambda off: pl.BlockSpec((rb, dk), lambda b, hh, n: (b * nb + n, off * h + hh))
    out_p, s_p = pl.pallas_call(
        _hg_prompt_kernel,
        grid=(bp, h, nb),
        in_specs=[col(0), col(1), col(2), col(3),
                  pl.BlockSpec((1, 1, dk), lambda b, hh, n: (hh, 0, 0)),
                  pl.BlockSpec((1, dk), lambda b, hh, n: (0, 0))],
        out_specs=[pl.BlockSpec((rb, dk), lambda b, hh, n: (b * nb + n, hh)),
                   pl.BlockSpec((1, 1, dk, dk), lambda b, hh, n: (b, hh, 0, 0))],
        out_shape=[jax.ShapeDtypeStruct((t, h * dk), BF16),
                   jax.ShapeDtypeStruct((bp, h, dk, dk), F32)],
        scratch_shapes=[pltpu.VMEM((dk, dk), F32)],
        compiler_params=_cparams(3),
        name="hgrn_prompt",
    )(qfig, qfig, qfig, qfig, lb3, gain2)

    sb = _pick_tile(bs, 8, 1)
    rows = sb * ls
    assert rows % 16 == 0 and (bp * lp) % rows == 0
    r0 = (bp * lp) // rows
    scol = lambda off: pl.BlockSpec((rows, dk), lambda i, hh: (r0 + i, off * h + hh))
    out_all, s_s = pl.pallas_call(
        functools.partial(_hg_sample_kernel, ls=ls),
        grid=(bs // sb, h),
        in_specs=[pl.BlockSpec(memory_space=pl.ANY),
                  scol(0), scol(1), scol(2), scol(3),
                  pl.BlockSpec((1, 1, dk), lambda i, hh: (hh, 0, 0)),
                  pl.BlockSpec((1, dk), lambda i, hh: (0, 0)),
                  pl.BlockSpec((sb, 1, dk, dk), lambda i, hh: (i, hh, 0, 0))],
        out_specs=[pl.BlockSpec((rows, dk), lambda i, hh: (r0 + i, hh)),
                   pl.BlockSpec((sb, 1, dk, dk), lambda i, hh: (i, hh, 0, 0))],
        out_shape=[jax.ShapeDtypeStruct((t, h * dk), BF16),
                   jax.ShapeDtypeStruct((bs, h, dk, dk), F32)],
        input_output_aliases={0: 0},
        compiler_params=_cparams(2),
        name="hgrn_sample",
    )(out_p, qfig, qfig, qfig, qfig, lb3, gain2, s0)
    return out_all, s_p, s_s


def _router_kernel(x_ref, g_ref, w_ref, b_ref, t_ref, idx_ref, gate_ref):
    x = x_ref[...]
    ms = jnp.mean(x * x, axis=-1, keepdims=True)
    tok = x * lax.rsqrt(ms + NORM_EPS) * g_ref[...]
    half = tok.shape[1] // 2
    lo = lax.bitcast_convert_type(tok[:, :half].astype(BF16).astype(F32), jnp.uint32)
    hi = lax.bitcast_convert_type(tok[:, half:].astype(BF16).astype(F32), jnp.uint32)
    packed = (lo >> 16) | (hi & jnp.uint32(0xFFFF0000))
    nl = half // LANES
    for k in range(nl):
        t_ref[pl.ds(k, tok.shape[0], stride=nl), :] = packed[:, k * LANES:(k + 1) * LANES]
    t_hi = tok.astype(BF16)
    t_lo = (tok - t_hi.astype(F32)).astype(BF16)
    w = w_ref[...]
    w_hi = w.astype(BF16)
    w_lo = (w - w_hi.astype(F32)).astype(BF16)
    logits = (jnp.dot(t_hi, w_hi, preferred_element_type=F32) + jnp.dot(t_hi, w_lo, preferred_element_type=F32)
              + jnp.dot(t_lo, w_hi, preferred_element_type=F32)) + b_ref[...]
    ne = logits.shape[1]
    lane = lax.broadcasted_iota(jnp.int32, logits.shape, 1).astype(F32)
    out_lane = lax.broadcasted_iota(jnp.int32, idx_ref.shape, 1)
    idx_out = jnp.zeros(idx_ref.shape, jnp.int32)
    val_out = jnp.zeros(idx_ref.shape, F32)
    vals = []
    cur = logits
    for r in range(TOP_K):
        m = jnp.max(cur, axis=-1, keepdims=True)
        sel = jnp.min(jnp.where(cur == m, lane, float(ne)), axis=-1, keepdims=True)
        cur = jnp.where(lane == sel, -jnp.inf, cur)
        vals.append(m)
        idx_out = jnp.where(out_lane == r, sel.astype(jnp.int32), idx_out)
    denom = sum(jnp.exp(v - vals[0]) for v in vals)
    for r in range(TOP_K):
        val_out = jnp.where(out_lane == r, jnp.exp(vals[r] - vals[0]) / denom, val_out)
    idx_ref[...] = idx_out
    gate_ref[...] = val_out


def _router(x1, g_ffn, w_router, b_router):
    t, d = x1.shape
    ne = w_router.shape[1]
    tm = _pick_tile(t, 256, 16)
    return pl.pallas_call(
        _router_kernel,
        grid=(t // tm,),
        in_specs=[pl.BlockSpec((tm, d), lambda i: (i, 0)), pl.BlockSpec((1, d), lambda i: (0, 0)),
                  pl.BlockSpec((d, ne), lambda i: (0, 0)), pl.BlockSpec((1, ne), lambda i: (0, 0))],
        out_specs=[pl.BlockSpec((tm * (d // 2 // LANES), LANES), lambda i: (i, 0)), pl.BlockSpec((tm, LANES), lambda i: (i, 0)),
                   pl.BlockSpec((tm, LANES), lambda i: (i, 0))],
        out_shape=[jax.ShapeDtypeStruct((t * (d // 2 // LANES), LANES), jnp.uint32), jax.ShapeDtypeStruct((t, LANES), jnp.int32),
                   jax.ShapeDtypeStruct((t, LANES), F32)],
        compiler_params=_cparams(1),
        name="router",
    )(x1, g_ffn.reshape(1, d), w_router, b_router.reshape(1, ne))


def _row_copy(src_hbm, dst_vmem, src_row, dst_row, sem):
    return pltpu.make_async_copy(src_hbm.at[pl.ds(src_row, 1), :], dst_vmem.at[pl.ds(dst_row, 1), :], sem)


def _slab_copy(src_hbm, dst_vmem, src_tok, dst_row, nl, sem):
    return pltpu.make_async_copy(src_hbm.at[pl.ds(pl.multiple_of(src_tok * nl, nl), nl), :],
                                 dst_vmem.at[pl.ds(pl.multiple_of(dst_row * nl, nl), nl), :], sem)


def _gather_kernel(cur_tab, nxt_tab, src_hbm, o_ref, buf0, buf1, sem, *, nl):
    bm = o_ref.shape[0]
    half = nl * LANES
    i = pl.program_id(0)
    bufs = (buf0, buf1)

    def issue(tab, slot):
        def start(r, c):
            _slab_copy(src_hbm, bufs[slot], tab[0, r], r, nl, sem.at[slot]).start()
            return c
        lax.fori_loop(0, bm, start, 0, unroll=8)

    def finish(slot):
        def wait(r, c):
            _slab_copy(src_hbm, bufs[slot], 0, r, nl, sem.at[slot]).wait()
            return c
        lax.fori_loop(0, bm, wait, 0, unroll=8)
        for k in range(nl):
            w = bufs[slot][pl.ds(k, bm, stride=nl), :]
            o_ref[:, k * LANES:(k + 1) * LANES] = lax.bitcast_convert_type(w << 16, F32).astype(o_ref.dtype)
            o_ref[:, half + k * LANES:half + (k + 1) * LANES] = (
                lax.bitcast_convert_type(w & jnp.uint32(0xFFFF0000), F32).astype(o_ref.dtype))

    @pl.when(i == 0)
    def _():
        issue(cur_tab, 0)

    for slot in range(2):
        @pl.when((i % 2 == slot) & (i + 1 < pl.num_programs(0)))
        def _(slot=slot):
            issue(nxt_tab, 1 - slot)

        @pl.when(i % 2 == slot)
        def _(slot=slot):
            finish(slot)


def _gather_rows(src, row_tok, bm, nl):
    p = row_tok.shape[0]
    d = 2 * nl * LANES
    nblk = p // bm
    tab = row_tok.reshape(nblk, 1, bm)
    return pl.pallas_call(
        functools.partial(_gather_kernel, nl=nl),
        grid=(nblk,),
        in_specs=[pl.BlockSpec((None, 1, bm), lambda i: (i, 0, 0), memory_space=pltpu.SMEM),
                  pl.BlockSpec((None, 1, bm), lambda i: (jnp.minimum(i + 1, nblk - 1), 0, 0), memory_space=pltpu.SMEM),
                  pl.BlockSpec(memory_space=pl.ANY)],
        out_specs=pl.BlockSpec((bm, d), lambda i: (i, 0)),
        scratch_shapes=[pltpu.VMEM((bm * nl, LANES), src.dtype), pltpu.VMEM((bm * nl, LANES), src.dtype),
                        pltpu.SemaphoreType.DMA((2,))],
        out_shape=jax.ShapeDtypeStruct((p, d), BF16),
        compiler_params=_cparams(1),
        name="moe_gather",
    )(tab, tab, src)


def _moe_group_kernel(e_tab, blk_tab, nb_tab, x_hbm, *refs, n_w):
    del e_tab
    w_refs = refs[:n_w]
    b_refs = refs[n_w:2 * n_w]
    o_hbm = refs[2 * n_w]
    xbuf, obuf, sem_x, sem_o = refs[2 * n_w + 1:]
    g, j = pl.program_id(0), pl.program_id(1)
    nb, blk0 = nb_tab[g], blk_tab[g]
    bm, tn = obuf.shape[1], obuf.shape[2]

    def rows_copy(b):
        return pltpu.make_async_copy(x_hbm.at[pl.ds((blk0 + b) * bm, bm), :],
                                     xbuf.at[pl.ds(b * bm, bm), :], sem_x)

    @pl.when((j == 0) & (nb > 0))
    def _():
        def start(b, c):
            rows_copy(b).start()
            return c

        def wait(b, c):
            rows_copy(b).wait()
            return c

        lax.fori_loop(0, nb, start, 0)
        lax.fori_loop(0, nb, wait, 0)

    def out_copy(b, slot):
        return pltpu.make_async_copy(obuf.at[slot], o_hbm.at[pl.ds((blk0 + b) * bm, bm), pl.ds(j * tn, tn)],
                                     sem_o.at[slot])

    @pl.when(nb > 0)
    def _():
        def body(b, c):
            slot = b % 2

            @pl.when(b >= 2)
            def _():
                out_copy(b - 2, slot).wait()

            x = xbuf[pl.ds(pl.multiple_of(b * bm, bm), bm), :]
            acc = jnp.dot(x, w_refs[0][...].astype(BF16), preferred_element_type=F32) + b_refs[0][...]
            if n_w == 2:
                up = jnp.dot(x, w_refs[1][...].astype(BF16), preferred_element_type=F32) + b_refs[1][...]
                gate = jnp.minimum(acc, SWIGLU_LIMIT)
                up = jnp.clip(up, -SWIGLU_LIMIT, SWIGLU_LIMIT)
                acc = (up + 1.0) * gate * _sigmoid(SWIGLU_ALPHA * gate)
            obuf[slot] = acc.astype(obuf.dtype)
            out_copy(b, slot).start()
            return c

        lax.fori_loop(0, nb, body, 0)

        @pl.when(nb >= 2)
        def _():
            out_copy(nb - 2, nb % 2).wait()

        out_copy(nb - 1, (nb - 1) % 2).wait()


def _moe_group_tables(nb_e, n_groups):
    ne = nb_e.shape[0]
    gb = MOE_GROUP_BLOCKS
    ng_e = (nb_e + gb - 1) // gb
    g_end = jnp.cumsum(ng_e)
    g_start = g_end - ng_e
    total = g_end[-1]
    blk_start = jnp.cumsum(nb_e) - nb_e
    gi = jnp.arange(n_groups, dtype=jnp.int32)
    gc = jnp.minimum(gi, total - 1)
    e = jnp.minimum(jnp.sum((gc[:, None] >= g_end[None, :]).astype(jnp.int32), axis=1), ne - 1)
    local = gc - g_start[e]
    blk0 = blk_start[e] + gb * local
    nb = jnp.where(gi < total, jnp.minimum(gb, nb_e[e] - gb * local), 0)
    return e.astype(jnp.int32), blk0.astype(jnp.int32), nb.astype(jnp.int32)


def _moe_grouped(x, tabs, ws, bs, out_dtype, bm, tn):
    p, k = x.shape
    ne, _, n = ws[0].shape
    n_w = len(ws)
    nj = n // tn
    n_groups = tabs[0].shape[0]
    frozen = lambda g, j, e, blk, nb: (e[g], 0, jnp.where(nb[g] > 0, j, nj - 1))
    return pl.pallas_call(
        functools.partial(_moe_group_kernel, n_w=n_w),
        grid_spec=pltpu.PrefetchScalarGridSpec(
            num_scalar_prefetch=3,
            grid=(n_groups, nj),
            in_specs=[pl.BlockSpec(memory_space=pl.ANY)]
            + [pl.BlockSpec((None, k, tn), frozen)] * n_w
            + [pl.BlockSpec((None, 1, tn), frozen)] * n_w,
            out_specs=pl.BlockSpec(memory_space=pl.ANY),
            scratch_shapes=[pltpu.VMEM((MOE_GROUP_BLOCKS * bm, k), x.dtype),
                            pltpu.VMEM((2, bm, tn), out_dtype), pltpu.SemaphoreType.DMA(()), pltpu.SemaphoreType.DMA((2,))],
        ),
        out_shape=jax.ShapeDtypeStruct((p, n), out_dtype),
        compiler_params=_cparams(2),
        name="moe_up" if n_w == 2 else "moe_down",
    )(*tabs, x, *ws, *[b.reshape(ne, 1, n) for b in bs])


def _moe_experts(xs, nb_e, w_gate, b_gate, w_up, b_up, w_down, b_down, bm):
    p, d = xs.shape
    ne, _, f = w_gate.shape
    n_groups = ne + -(-(p // bm) // MOE_GROUP_BLOCKS)
    tabs = _moe_group_tables(nb_e, n_groups)
    hid = _moe_grouped(xs, tabs, (w_gate, w_up), (b_gate, b_up), BF16, bm, min(512, f))
    return _moe_grouped(hid, tabs, (w_down,), (b_down,), F32, bm, min(1024, d))


def _combine_kernel(cur_tab, nxt_tab, y_hbm, x_ref, gate_ref, g_ref, op_ref, os_ref, buf, sem, *, npb):
    tc = x_ref.shape[0]
    i = pl.program_id(0)

    def issue(tab, slot):
        def start(r, c):
            for k in range(TOP_K):
                _row_copy(y_hbm, buf.at[slot, k], tab[0, r * TOP_K + k], r, sem.at[slot]).start(priority=k % 2)
            return c
        lax.fori_loop(0, tc, start, 0, unroll=2)

    @pl.when(i == 0)
    def _():
        issue(cur_tab, 0)

    @pl.when(i + 1 < pl.num_programs(0))
    def _():
        issue(nxt_tab, (i + 1) % 2)

    slot = i % 2

    def wait(r, c):
        for k in range(TOP_K):
            _row_copy(y_hbm, buf.at[slot, k], 0, r, sem.at[slot]).wait()
        return c

    lax.fori_loop(0, tc, wait, 0, unroll=2)
    gates = gate_ref[...]
    acc = x_ref[...]
    for k in range(TOP_K):
        acc = acc + gates[:, k:k + 1] * buf[slot, k]
    ms = jnp.mean(acc * acc, axis=-1, keepdims=True)
    out = acc * lax.rsqrt(ms + NORM_EPS) * g_ref[...]

    @pl.when(i < npb)
    def _():
        op_ref[...] = out

    @pl.when(i >= npb)
    def _():
        os_ref[...] = out


def _combine(y, pos, x1, gates, g_final, tp):
    t, d = x1.shape
    ts = t - tp
    tc = _pick_tile(math.gcd(tp, ts), 64, 8)
    npb = tp // tc
    nblk = t // tc
    tab = pos.reshape(nblk, 1, tc * TOP_K)
    return pl.pallas_call(
        functools.partial(_combine_kernel, npb=npb),
        grid=(nblk,),
        in_specs=[pl.BlockSpec((None, 1, tc * TOP_K), lambda i: (i, 0, 0), memory_space=pltpu.SMEM),
                  pl.BlockSpec((None, 1, tc * TOP_K), lambda i: (jnp.minimum(i + 1, nblk - 1), 0, 0),
                               memory_space=pltpu.SMEM),
                  pl.BlockSpec(memory_space=pl.ANY),
                  pl.BlockSpec((tc, d), lambda i: (i, 0)),
                  pl.BlockSpec((tc, LANES), lambda i: (i, 0)),
                  pl.BlockSpec((1, d), lambda i: (0, 0))],
        out_specs=[pl.BlockSpec((tc, d), lambda i: (jnp.minimum(i, npb - 1), 0)),
                   pl.BlockSpec((tc, d), lambda i: (jnp.maximum(i - npb, 0), 0))],
        scratch_shapes=[pltpu.VMEM((2, TOP_K, tc, d), F32), pltpu.SemaphoreType.DMA((2,))],
        out_shape=[jax.ShapeDtypeStruct((tp, d), F32), jax.ShapeDtypeStruct((ts, d), F32)],
        compiler_params=_cparams(1),
        name="moe_combine",
    )(tab, tab, y, x1, gates, g_final.reshape(1, d))


def kernel(x_prompt, x_sample, state_ssm_re, state_ssm_im, state_hgrn, g_mix, w_in, ssm_a_re, ssm_a_im, ssm_log_dt, ssm_b_re, ssm_b_im, ssm_c_re, ssm_c_im, ssm_d, w_glu, b_glu, hg_lb_logits, hg_o_gain, w_out, g_ffn, w_router, b_router, w_gate, b_gate, w_up, b_up, w_down, b_down, g_final):
    bp, lp, d = x_prompt.shape
    bs, ls, _ = x_sample.shape
    depth = w_in.shape[0]
    assert depth == 1
    tp, ts = bp * lp, bs * ls
    t = tp + ts
    s5w = ssm_b_re.shape[1] * ssm_b_re.shape[3]
    h, dk = hg_lb_logits.shape[1:]
    ne = w_router.shape[2]

    x0 = (x_prompt.reshape(tp, d), x_sample.reshape(ts, d))
    lower_bounds = jnp.cumsum(jax.nn.softmax(hg_lb_logits.astype(F32), axis=0), axis=0)

    hn = _rmsnorm(x0[0], x0[1], g_mix[0], BF16)
    u_fold = _matmul([hn], w_in[0], 0, s5w, fold_out=True, tm_target=1280)
    qfig = _matmul([hn], w_in[0], s5w, w_in.shape[2] - s5w, tm_target=1280)
    ssm = (ssm_a_re[0], ssm_a_im[0], ssm_log_dt[0], ssm_b_re[0], ssm_b_im[0], ssm_c_re[0], ssm_c_im[0], ssm_d[0])
    z_fold, re_p, im_p, re_s, im_s = _s5_mixer(u_fold, ssm, state_ssm_re[0], state_ssm_im[0], bp, lp, bs, ls)
    s5_out = _glu(z_fold, w_glu[0], b_glu[0])
    hg_out, s_p, s_s = _hgrn_mixer(qfig, lower_bounds[0], hg_o_gain[0], state_hgrn[0], bp, lp, bs, ls)
    x1 = _matmul([s5_out, hg_out], w_out[0], 0, d, res=x0)

    tok, idx, gates = _router(x1, g_ffn[0], w_router[0], b_router[0])
    m = t * TOP_K
    bm = MOE_BLOCK_ROWS
    flat_e = idx[:, :TOP_K].reshape(m)
    onehot = (flat_e[:, None] == jnp.arange(ne, dtype=jnp.int32)[None, :]).astype(jnp.int32)
    seen = jnp.cumsum(onehot, axis=0)
    counts = seen[-1]
    nb_e = (counts + bm - 1) // bm
    pad_end = jnp.cumsum(nb_e * bm)
    pad_start = pad_end - nb_e * bm
    pos = jnp.sum(onehot * (seen - 1 + pad_start[None, :]), axis=1).astype(jnp.int32)
    n_blocks = -(-(m + ne * (bm - 1)) // bm)
    p_rows = n_blocks * bm
    row_tok = jnp.zeros((p_rows,), jnp.int32).at[pos].set(jnp.arange(m, dtype=jnp.int32) // TOP_K)
    xs = _gather_rows(tok, row_tok, bm, d // 2 // LANES)
    y_rows = _moe_experts(xs, nb_e, w_gate[0], b_gate[0], w_up[0], b_up[0], w_down[0], b_down[0], bm)
    y_p, y_s = _combine(y_rows, pos, x1, gates, g_final, tp)

    y_prompt = y_p.reshape(bp, lp, d)
    y_sample = y_s.reshape(bs, ls, d)
    sd = state_ssm_re.dtype
    return (y_prompt, y_sample, re_p[None].astype(sd), im_p[None].astype(sd), s_p[None].astype(state_hgrn.dtype),
            re_s[None].astype(sd), im_s[None].astype(sd), s_s[None].astype(state_hgrn.dtype))
```
